```python
import math
import jax, jax.numpy as jnp
from jax import lax
import numpy as np

D_MODEL = 2048
BATCH = 2
SEQ = 8192
DEPTH = 4

N_MIXERS = 4
MIX_WIDTH = D_MODEL
GROUP_WIDTH = MIX_WIDTH // N_MIXERS
HEAD_DIM = 128
N_HEADS = GROUP_WIDTH // HEAD_DIM
GLA_DK = HEAD_DIM // 2
GLA_DV = HEAD_DIM
GLA_RANK = 16
GLA_GATE_NORM = 16.0
GLA_CHUNK = 64
SWA_KV_HEADS = N_HEADS // 2
SWA_WINDOW = 128
DIFF_DQK = HEAD_DIM // 2
DIFF_BLOCK = 128
GDN_CONV = 4
GDN_CHUNK = 64
NORM_EPS = 1e-6

PROJ_SIZES = (
    N_HEADS * GLA_DK, N_HEADS * GLA_DK, N_HEADS * GLA_DV, GLA_RANK, GROUP_WIDTH,
    N_HEADS * HEAD_DIM, SWA_KV_HEADS * HEAD_DIM, SWA_KV_HEADS * HEAD_DIM, GROUP_WIDTH,
    N_HEADS * 2 * DIFF_DQK, N_HEADS * 2 * DIFF_DQK, N_HEADS * HEAD_DIM, GROUP_WIDTH,
    3 * GROUP_WIDTH, N_HEADS, N_HEADS, GROUP_WIDTH,
)
PROJ_WIDTH = 7192

kernel_name = "hybrid_parallel_heads_gla_swa_diff_gdn"


def rms_norm(x, g):
    xf = x.astype(jnp.float32)
    y = xf * lax.rsqrt(jnp.mean(xf * xf, axis=-1, keepdims=True) + NORM_EPS)
    return (y * g.astype(jnp.float32)).astype(x.dtype)


def l2_norm(x):
    return x * lax.rsqrt(jnp.sum(x * x, axis=-1, keepdims=True) + NORM_EPS)


def alibi_slopes():
    n = 2 * N_HEADS
    s = jnp.asarray(2.0 ** (-8.0 * np.arange(1, n + 1) / n), dtype=jnp.float32)
    return s[0::2], s[1::2]


def causal_conv(x, w):
    K, C = w.shape
    return lax.conv_general_dilated(x, w[:, None, :], window_strides=(1,), padding=[(K - 1, 0)],
                                    dimension_numbers=('NWC', 'WIO', 'NWC'), feature_group_count=C)


def gla_mixer(q, k, v, lr, w_lr2, b_lr2, norm_g):
    dt = q.dtype
    Bn, T, _ = q.shape
    C = GLA_CHUNK
    N = T // C
    g = jax.nn.log_sigmoid((lr @ w_lr2 + b_lr2).astype(jnp.float32)) / GLA_GATE_NORM

    def ch(t, d):
        return t.astype(jnp.float32).reshape(Bn, N, C, N_HEADS, d).transpose(0, 3, 1, 2, 4)

    qc = ch(q, GLA_DK) * GLA_DK ** -0.5
    kc = ch(k, GLA_DK)
    vc = ch(v, GLA_DV)
    b = jnp.cumsum(ch(g, GLA_DK), axis=3)
    b_last = b[:, :, :, -1:, :]
    q_dec = qc * jnp.exp(b)
    k_inv = kc * jnp.exp(-b)
    causal = jnp.tril(jnp.ones((C, C), dtype=bool))
    a = jnp.where(causal, jnp.einsum('bhncd,bhnsd->bhncs', q_dec, k_inv), 0.0)
    o_intra = jnp.einsum('bhncs,bhnsv->bhncv', a, vc)
    u = jnp.einsum('bhncd,bhncv->bhndv', kc * jnp.exp(b_last - b), vc)
    decay = jnp.exp(b_last[:, :, :, 0, :])

    def step(S, inp):
        u_n, d_n = inp
        return d_n[..., None] * S + u_n, S

    S0 = jnp.zeros((Bn, N_HEADS, GLA_DK, GLA_DV), jnp.float32)
    _, S_prev = lax.scan(step, S0, (jnp.moveaxis(u, 2, 0), jnp.moveaxis(decay, 2, 0)))
    o_inter = jnp.einsum('bhncd,nbhdv->bhncv', q_dec, S_prev)
    o = rms_norm(o_intra + o_inter, norm_g)
    return o.transpose(0, 2, 3, 1, 4).reshape(Bn, T, N_HEADS * GLA_DV).astype(dt)


def swa_mixer(q, k, v, q_g, k_g, sinks, slopes):
    dt = q.dtype
    Bn, T, _ = q.shape
    W = SWA_WINDOW
    N = T // W
    G = SWA_KV_HEADS
    R = N_HEADS // G
    d = HEAD_DIM
    q = rms_norm(q.astype(jnp.float32).reshape(Bn, N, W, G, R, d), q_g).transpose(0, 3, 4, 1, 2, 5)
    k = rms_norm(k.astype(jnp.float32).reshape(Bn, N, W, G, d), k_g).transpose(0, 3, 1, 2, 4)
    v = v.astype(jnp.float32).reshape(Bn, N, W, G, d).transpose(0, 3, 1, 2, 4)

    def band(t):
        prev = jnp.pad(t, ((0, 0), (0, 0), (1, 0), (0, 0), (0, 0)))[:, :, :-1]
        return jnp.concatenate([prev, t], axis=3)

    kb, vb = band(k), band(v)
    s = jnp.einsum('bgrnid,bgnjd->bgrnij', q, kb) * d ** -0.5
    i = jnp.arange(W)[:, None]
    j = jnp.arange(2 * W)[None, :]
    dist = i + W - j
    key_pos = jnp.arange(N)[:, None, None] * W + j[None] - W
    valid = (dist >= 0) & (dist < W) & (key_pos >= 0)
    m_h = slopes.reshape(G, R, 1, 1, 1)
    s = jnp.where(valid, s - m_h * dist.astype(jnp.float32), -jnp.inf)
    sink = sinks.astype(jnp.float32).reshape(1, G, R, 1, 1, 1)
    mx = jnp.maximum(jnp.max(s, axis=-1, keepdims=True), sink)
    p = jnp.exp(s - mx)
    p = p / (jnp.sum(p, axis=-1, keepdims=True) + jnp.exp(sink - mx))
    o = jnp.einsum('bgrnij,bgnjd->bgrnid', p, vb)
    return o.transpose(0, 3, 4, 1, 2, 5).reshape(Bn, T, N_HEADS * d).astype(dt)


def diff_mixer(q, k, v, q_g, k_g, lam, subln_g, slopes, lambda_init):
    dt = q.dtype
    Bn, T, _ = q.shape
    H = N_HEADS
    dq = DIFF_DQK
    BLK = DIFF_BLOCK
    N = T // BLK
    q = rms_norm(q.astype(jnp.float32).reshape(Bn, T, H, 2, dq), q_g).transpose(0, 2, 3, 1, 4) * dq ** -0.5
    k = rms_norm(k.astype(jnp.float32).reshape(Bn, T, H, 2, dq), k_g).transpose(0, 2, 3, 1, 4)
    v = v.astype(jnp.float32).reshape(Bn, T, H, HEAD_DIM).transpose(0, 2, 1, 3)
    lam = lam.astype(jnp.float32)
    lam_full = jnp.exp(jnp.sum(lam[0] * lam[1])) - jnp.exp(jnp.sum(lam[2] * lam[3])) + lambda_init
    qb = jnp.moveaxis(q.reshape(Bn, H, 2, N, BLK, dq), 3, 0)
    key_pos = jnp.arange(T)
    m_h = slopes.reshape(1, H, 1, 1, 1)

    def block(args):
        q_blk, n = args
        s = jnp.einsum('bhmid,bhmjd->bhmij', q_blk, k)
        dist = (n * BLK + jnp.arange(BLK))[:, None] - key_pos[None, :]
        s = jnp.where(dist >= 0, s - m_h * dist.astype(jnp.float32), -jnp.inf)
        p = jax.nn.softmax(s, axis=-1)
        w = p[:, :, 0] - lam_full * p[:, :, 1]
        return jnp.einsum('bhij,bhjd->bhid', w, v)

    o = lax.map(block, (qb, jnp.arange(N)))
    o = rms_norm(o, subln_g) * (1.0 - lambda_init)
    o = jnp.moveaxis(o, 0, 2).reshape(Bn, H, T, HEAD_DIM).transpose(0, 2, 1, 3)
    return o.reshape(Bn, T, H * HEAD_DIM).astype(dt)


def gdn_mixer(qkv, beta_raw, a_raw, conv_w, A_log, dt_bias, norm_g):
    dt = qkv.dtype
    Bn, T, _ = qkv.shape
    H = N_HEADS
    d = HEAD_DIM
    C = GDN_CHUNK
    N = T // C
    qkv = jax.nn.silu(causal_conv(qkv, conv_w)).astype(jnp.float32)
    q, k, v = jnp.split(qkv, 3, axis=-1)

    def ch(t):
        return t.reshape(Bn, N, C, H, d).transpose(0, 3, 1, 2, 4)

    qc = l2_norm(ch(q)) * d ** -0.5
    kc = l2_norm(ch(k))
    vc = ch(v)
    beta = jax.nn.sigmoid(beta_raw.astype(jnp.float32)).reshape(Bn, N, C, H).transpose(0, 3, 1, 2)
    g = -jnp.exp(A_log.astype(jnp.float32)) * jax.nn.softplus(a_raw.astype(jnp.float32) + dt_bias.astype(jnp.float32))
    g = g.reshape(Bn, N, C, H).transpose(0, 3, 1, 2)
    dec = jnp.cumsum(g, axis=-1)
    tril = jnp.tril(jnp.ones((C, C), dtype=bool))
    strict = jnp.tril(jnp.ones((C, C), dtype=bool), -1)
    diff = dec[..., :, None] - dec[..., None, :]
    dmask = jnp.where(tril, jnp.exp(jnp.where(tril, diff, 0.0)), 0.0)
    kb = kc * beta[..., None]
    M = jnp.where(strict, jnp.einsum('bhncd,bhnsd->bhncs', kb, kc) * dmask, 0.0)
    eye = jnp.eye(C, dtype=jnp.float32)
    Tm = lax.linalg.triangular_solve(M + eye, jnp.broadcast_to(eye, M.shape), left_side=True,
                                     lower=True, unit_diagonal=True)
    u = jnp.einsum('bhncs,bhnsv->bhncv', Tm, vc * beta[..., None])
    w = jnp.einsum('bhncs,bhnsk->bhnck', Tm, kb * jnp.exp(dec)[..., None])
    qk = jnp.einsum('bhncd,bhnsd->bhncs', qc, kc) * dmask
    q_dec = qc * jnp.exp(dec)[..., None]
    k_end = kc * jnp.exp(dec[..., -1:] - dec)[..., None]
    d_last = jnp.exp(dec[..., -1])

    def step(S, inp):
        u_n, w_n, qd_n, qk_n, ke_n, dl_n = inp
        v_new = u_n - jnp.einsum('bhck,bhkv->bhcv', w_n, S)
        o_n = jnp.einsum('bhck,bhkv->bhcv', qd_n, S) + jnp.einsum('bhcs,bhsv->bhcv', qk_n, v_new)
        S = S * dl_n[..., None, None] + jnp.einsum('bhck,bhcv->bhkv', ke_n, v_new)
        return S, o_n

    S0 = jnp.zeros((Bn, H, d, d), jnp.float32)
    xs = (jnp.moveaxis(u, 2, 0), jnp.moveaxis(w, 2, 0), jnp.moveaxis(q_dec, 2, 0),
          jnp.moveaxis(qk, 2, 0), jnp.moveaxis(k_end, 2, 0), jnp.moveaxis(d_last, 2, 0))
    _, o = lax.scan(step, S0, xs)
    o = rms_norm(o, norm_g)
    return o.transpose(1, 0, 3, 2, 4).reshape(Bn, T, H * d).astype(dt)


def hybrid_layer(x, norm_g, w_in, w_out, gla_w_lr2, gla_b_lr2, gla_norm_g, swa_q_norm_g, swa_k_norm_g,
                 swa_sinks, diff_q_norm_g, diff_k_norm_g, diff_lambda, diff_subln_g, gdn_conv_w, gdn_A_log,
                 gdn_dt_bias, gdn_norm_g, lambda_init):
    h = rms_norm(x, norm_g)
    p = jnp.einsum('btd,dp->btp', h, w_in)
    offs = np.cumsum(np.array(PROJ_SIZES))[:-1].tolist()
    (aq, ak, av, alr, az, bq, bk, bv, bz, cq, ck, cv, cz, dqkv, dbeta, da, dz) = jnp.split(p, offs, axis=-1)
    slopes_b, slopes_c = alibi_slopes()
    ya = gla_mixer(aq, ak, av, alr, gla_w_lr2, gla_b_lr2, gla_norm_g) * jax.nn.silu(az)
    yb = swa_mixer(bq, bk, bv, swa_q_norm_g, swa_k_norm_g, swa_sinks, slopes_b) * jax.nn.silu(bz)
    yc = diff_mixer(cq, ck, cv, diff_q_norm_g, diff_k_norm_g, diff_lambda, diff_subln_g, slopes_c,
                    lambda_init) * jax.nn.silu(cz)
    yd = gdn_mixer(dqkv, dbeta, da, gdn_conv_w, gdn_A_log, gdn_dt_bias, gdn_norm_g) * jax.nn.silu(dz)
    y = jnp.concatenate([ya, yb, yc, yd], axis=-1)
    return x + jnp.einsum('btm,md->btd', y, w_out)


def setup_inputs(seed: int = 0) -> dict:
    key = jax.random.key(seed)
    ks = jax.random.split(key, 20)
    f32 = jnp.float32
    nrm = lambda k, shape: jax.random.normal(k, shape, f32)
    dt = jnp.exp(jax.random.uniform(ks[17], (DEPTH, N_HEADS), f32) * (math.log(0.1) - math.log(0.001)) + math.log(0.001))
    return {
        'x': nrm(ks[0], (BATCH, SEQ, D_MODEL)),
        'norm_g': 1.0 + 0.02 * nrm(ks[1], (DEPTH, D_MODEL)),
        'w_in': nrm(ks[2], (DEPTH, D_MODEL, PROJ_WIDTH)) * D_MODEL ** -0.5,
        'w_out': nrm(ks[3], (DEPTH, MIX_WIDTH, D_MODEL)) * MIX_WIDTH ** -0.5,
        'gla_w_lr2': nrm(ks[4], (DEPTH, GLA_RANK, N_HEADS * GLA_DK)) * GLA_RANK ** -0.5,
        'gla_b_lr2': 0.1 * nrm(ks[5], (DEPTH, N_HEADS * GLA_DK)),
        'gla_norm_g': 1.0 + 0.02 * nrm(ks[6], (DEPTH, GLA_DV)),
        'swa_q_norm_g': 1.0 + 0.02 * nrm(ks[7], (DEPTH, HEAD_DIM)),
        'swa_k_norm_g': 1.0 + 0.02 * nrm(ks[8], (DEPTH, HEAD_DIM)),
        'swa_sinks': nrm(ks[9], (DEPTH, N_HEADS)),
        'diff_q_norm_g': 1.0 + 0.02 * nrm(ks[10], (DEPTH, DIFF_DQK)),
        'diff_k_norm_g': 1.0 + 0.02 * nrm(ks[11], (DEPTH, DIFF_DQK)),
        'diff_lambda': 0.1 * nrm(ks[12], (DEPTH, 4, DIFF_DQK)),
        'diff_subln_g': 1.0 + 0.02 * nrm(ks[13], (DEPTH, HEAD_DIM)),
        'gdn_conv_w': nrm(ks[14], (DEPTH, GDN_CONV, 3 * GROUP_WIDTH)) * GDN_CONV ** -0.5,
        'gdn_A_log': jnp.log(jax.random.uniform(ks[15], (DEPTH, N_HEADS), f32, 1.0, 16.0)),
        'gdn_dt_bias': dt + jnp.log(-jnp.expm1(-dt)),
        'gdn_norm_g': 1.0 + 0.02 * nrm(ks[16], (DEPTH, HEAD_DIM)),
    }


def reference(x, norm_g, w_in, w_out, gla_w_lr2, gla_b_lr2, gla_norm_g, swa_q_norm_g, swa_k_norm_g, swa_sinks,
              diff_q_norm_g, diff_k_norm_g, diff_lambda, diff_subln_g, gdn_conv_w, gdn_A_log, gdn_dt_bias,
              gdn_norm_g):
    for l in range(DEPTH):
        lambda_init = 0.8 - 0.6 * math.exp(-0.3 * l)
        x = hybrid_layer(x, norm_g[l], w_in[l], w_out[l], gla_w_lr2[l], gla_b_lr2[l], gla_norm_g[l],
                         swa_q_norm_g[l], swa_k_norm_g[l], swa_sinks[l], diff_q_norm_g[l], diff_k_norm_g[l],
                         diff_lambda[l], diff_subln_g[l], gdn_conv_w[l], gdn_A_log[l], gdn_dt_bias[l],
                         gdn_norm_g[l], lambda_init)
    return x
```

```python
import functools
import math

import numpy as np
import jax
import jax.numpy as jnp
from jax import lax
from jax.experimental import pallas as pl
from jax.experimental.pallas import tpu as pltpu

F32 = jnp.float32
BF16 = jnp.bfloat16

D_MODEL = 2048
N_HEADS = 4
HEAD_DIM = 128
GROUP_WIDTH = 512
GLA_DK = 64
GLA_RANK = 16
GLA_GATE_NORM = 16.0
CHUNK = 64
SWA_WINDOW = 128
DIFF_DQK = 64
GDN_CONV = 4
NORM_EPS = 1e-6
LANES = 128
VMEM_LIMIT = 56 * 1024 * 1024

_REF_SIZES = (("aq", 256), ("ak", 256), ("av", 512), ("alr", 16), ("az", 512),
              ("bq", 512), ("bk", 256), ("bv", 256), ("bz", 512),
              ("cq", 512), ("ck", 512), ("cv", 512), ("cz", 512),
              ("dqkv", 1536), ("dbeta", 4), ("da", 4), ("dz", 512))
_REF_OFF = {}
_o = 0
for _n, _w in _REF_SIZES:
    _REF_OFF[_n] = (_o, _w)
    _o += _w
PROJ_WIDTH = _o

_MAIN_ORDER = ("dqkv", "dz", "aq", "ak", "av", "az", "bq", "bk", "bv", "bz", "cq", "ck", "cv", "cz")
_MAIN_OFF = {}
_o = 0
for _n in _MAIN_ORDER:
    _MAIN_OFF[_n] = _o
    _o += _REF_OFF[_n][1]
MAIN_WIDTH = _o
SMALL_LR, SMALL_BETA, SMALL_A = 0, 16, 20


def _blk(name, width):
    off = _MAIN_OFF[name]
    assert off % width == 0
    return off // width


def _silu(x):
    return x / (1.0 + jnp.exp(-x))


def _softplus(x):
    return jnp.maximum(x, 0.0) + jnp.log(1.0 + jnp.exp(-jnp.abs(x)))


def _rms(x, gain):
    return x * lax.rsqrt(jnp.mean(x * x, axis=-1, keepdims=True) + NORM_EPS) * gain


def _dot(a, b):
    return jnp.dot(a, b, preferred_element_type=F32)


def _dot_nt(a, b):
    return lax.dot_general(a, b, (((1,), (1,)), ((), ())), preferred_element_type=F32)


def _dot_tn(a, b):
    return lax.dot_general(a, b, (((0,), (0,)), ((), ())), preferred_element_type=F32)


def _split3(x):
    hi = x.astype(BF16)
    r1 = x - hi.astype(F32)
    mid = r1.astype(BF16)
    lo = (r1 - mid.astype(F32)).astype(BF16)
    return hi, mid, lo


def _cumsum_rows(tri, x):
    hi, mid, lo = _split3(x)
    return _dot(tri, hi) + _dot(tri, mid) + _dot(tri, lo)


def _tri_masks(n):
    r = lax.broadcasted_iota(jnp.int32, (n, n), 0)
    c = lax.broadcasted_iota(jnp.int32, (n, n), 1)
    return r >= c, r > c


def _inproj_kernel(x_ref, g_ref, w_ref, ws_ref, p_ref, ps_ref, h_ref):
    @pl.when(pl.program_id(1) == 0)
    def _():
        x = x_ref[...]
        h = _rms(x, g_ref[...]).astype(BF16)
        h_ref[...] = h
        ps_ref[...] = _dot(h, ws_ref[...])

    p_ref[...] = _dot(h_ref[...], w_ref[...]).astype(BF16)


def _inproj(x2, gain, w_main, w_small, *, tm=1024, tn=512):
    m, d = x2.shape
    n = w_main.shape[1]
    return pl.pallas_call(
        _inproj_kernel,
        grid=(m // tm, n // tn),
        in_specs=[pl.BlockSpec((tm, d), lambda i, j: (i, 0)),
                  pl.BlockSpec((1, d), lambda i, j: (0, 0)),
                  pl.BlockSpec((d, tn), lambda i, j: (0, j)),
                  pl.BlockSpec((d, LANES), lambda i, j: (0, 0))],
        out_specs=[pl.BlockSpec((tm, tn), lambda i, j: (i, j)),
                   pl.BlockSpec((tm, LANES), lambda i, j: (i, 0))],
        out_shape=[jax.ShapeDtypeStruct((m, n), BF16), jax.ShapeDtypeStruct((m, LANES), F32)],
        scratch_shapes=[pltpu.VMEM((tm, d), BF16)],
        compiler_params=pltpu.CompilerParams(dimension_semantics=("parallel", "arbitrary"),
                                             vmem_limit_bytes=VMEM_LIMIT),
        name="inproj",
    )(x2, gain, w_main, w_small)


def _outproj_kernel(ya_ref, yb_ref, yc_ref, yd_ref, x_ref, w_ref, o_ref):
    acc = x_ref[...]
    for g, y_ref in enumerate((ya_ref, yb_ref, yc_ref, yd_ref)):
        acc = acc + _dot(y_ref[...], w_ref[g * GROUP_WIDTH:(g + 1) * GROUP_WIDTH, :])
    o_ref[...] = acc


def _outproj(ys, x2, w_out, *, tm=512, tn=1024):
    m, d = x2.shape
    yspec = pl.BlockSpec((tm, GROUP_WIDTH), lambda i, j: (i, 0))
    return pl.pallas_call(
        _outproj_kernel,
        grid=(m // tm, d // tn),
        in_specs=[yspec, yspec, yspec, yspec,
                  pl.BlockSpec((tm, tn), lambda i, j: (i, j)),
                  pl.BlockSpec((w_out.shape[0], tn), lambda i, j: (0, j))],
        out_specs=pl.BlockSpec((tm, tn), lambda i, j: (i, j)),
        out_shape=jax.ShapeDtypeStruct((m, d), F32),
        compiler_params=pltpu.CompilerParams(dimension_semantics=("parallel", "arbitrary"),
                                             vmem_limit_bytes=VMEM_LIMIT),
        name="outproj",
    )(*ys, x2, w_out)


def _gla_kernel(q_ref, k_ref, v_ref, z_ref, ps_ref, wlr_ref, blr_ref, ng_ref, o_ref, st_ref, g_ref,
                *, blk):
    @pl.when(pl.program_id(1) == 0)
    def _():
        st_ref[...] = jnp.zeros_like(st_ref)

    logits = _dot(ps_ref[0].astype(BF16), wlr_ref[...]) + blr_ref[...]
    g_ref[...] = -_softplus(-logits) * (1.0 / GLA_GATE_NORM)

    tril, _ = _tri_masks(CHUNK)
    tri = jnp.where(tril, 1.0, 0.0).astype(BF16)
    ng = ng_ref[...]

    def chunk(c, carry):
        r0 = pl.multiple_of(c * CHUNK, CHUNK)
        rows = pl.ds(r0, CHUNK)
        b = _cumsum_rows(tri, g_ref[rows, :])
        b_last = b[CHUNK - 1:CHUNK, :]
        qf = q_ref[0, rows, :].astype(F32)
        kf = k_ref[0, rows, :].astype(F32)
        q_dec = qf * (GLA_DK ** -0.5) * jnp.exp(b)
        k_inv = kf * jnp.exp(-b)
        k_end = kf * jnp.exp(b_last - b)
        decay = jnp.exp(b_last)
        for h in range(N_HEADS):
            ks = slice(h * GLA_DK, (h + 1) * GLA_DK)
            vs = slice(h * HEAD_DIM, (h + 1) * HEAD_DIM)
            qd_h = q_dec[:, ks].astype(BF16)
            ki_h = k_inv[:, ks].astype(BF16)
            ke_h = k_end[:, ks].astype(BF16)
            v_h = v_ref[0, rows, vs]
            a = jnp.where(tril, _dot_nt(qd_h, ki_h), 0.0).astype(BF16)
            st = st_ref[h]
            o = _dot(a, v_h) + _dot_nt(qd_h, st.astype(BF16))
            st_ref[h] = st * decay[:, ks] + _dot_tn(v_h, ke_h)
            zf = z_ref[0, rows, vs].astype(F32)
            o_ref[0, rows, vs] = (_rms(o, ng) * _silu(zf)).astype(BF16)
        return carry

    lax.fori_loop(0, blk // CHUNK, chunk, 0)


def _gla(p3, ps3, wlr, blr, ng, *, blk=512):
    bsz, t, _ = p3.shape
    kern = functools.partial(_gla_kernel, blk=blk)
    return pl.pallas_call(
        kern,
        grid=(bsz, t // blk),
        in_specs=[pl.BlockSpec((1, blk, 256), lambda b, i: (b, i, _blk("aq", 256))),
                  pl.BlockSpec((1, blk, 256), lambda b, i: (b, i, _blk("ak", 256))),
                  pl.BlockSpec((1, blk, 512), lambda b, i: (b, i, _blk("av", 512))),
                  pl.BlockSpec((1, blk, 512), lambda b, i: (b, i, _blk("az", 512))),
                  pl.BlockSpec((1, blk, LANES), lambda b, i: (b, i, 0)),
                  pl.BlockSpec((LANES, 256), lambda b, i: (0, 0)),
                  pl.BlockSpec((1, 256), lambda b, i: (0, 0)),
                  pl.BlockSpec((1, HEAD_DIM), lambda b, i: (0, 0))],
        out_specs=pl.BlockSpec((1, blk, GROUP_WIDTH), lambda b, i: (b, i, 0)),
        out_shape=jax.ShapeDtypeStruct((bsz, t, GROUP_WIDTH), BF16),
        scratch_shapes=[pltpu.VMEM((N_HEADS, HEAD_DIM, GLA_DK), F32), pltpu.VMEM((blk, 256), F32)],
        compiler_params=pltpu.CompilerParams(dimension_semantics=("parallel", "arbitrary"),
                                             vmem_limit_bytes=VMEM_LIMIT),
        name="gla",
    )(p3, p3, p3, p3, ps3, wlr, blr, ng)


def _swa_kernel(slopes_ref, sinks_ref, q_ref, k_ref, kp_ref, v_ref, vp_ref, z_ref, qg_ref, kg_ref,
                o_ref, *, tq):
    w = SWA_WINDOW
    g = pl.program_id(1)
    i = pl.program_id(2)
    kc = _rms(k_ref[0].astype(F32), kg_ref[...]).astype(BF16)
    kp = _rms(kp_ref[0].astype(F32), kg_ref[...]).astype(BF16)
    row = lax.broadcasted_iota(jnp.int32, (2 * w, 1), 0)
    first = row < w
    slope = jnp.where(first, slopes_ref[2 * g], slopes_ref[2 * g + 1])
    sink = jnp.where(first, sinks_ref[2 * g], sinks_ref[2 * g + 1])
    qi = jnp.bitwise_and(lax.broadcasted_iota(jnp.int32, (2 * w, 2 * w), 0), w - 1)
    kj = lax.broadcasted_iota(jnp.int32, (2 * w, 2 * w), 1)
    dist = qi + w - kj
    in_win = lax.bitcast_convert_type(dist, jnp.uint32) < w
    bias = slope * dist.astype(F32)
    for n in range(tq // w):
        rows = slice(n * w, (n + 1) * w)
        k_prev = kp if n == 0 else kc[(n - 1) * w:n * w]
        v_prev = vp_ref[0] if n == 0 else v_ref[0, (n - 1) * w:n * w, :]
        kb = jnp.concatenate([k_prev, kc[rows]], axis=0)
        vb = jnp.concatenate([v_prev, v_ref[0, rows, :]], axis=0)
        qf = q_ref[0, rows, :].astype(F32)
        qs = jnp.concatenate([qf[:, :HEAD_DIM], qf[:, HEAD_DIM:]], axis=0)
        qs = (_rms(qs, qg_ref[...]) * (HEAD_DIM ** -0.5)).astype(BF16)
        key_pos = (i * (tq // w) + n - 1) * w + kj
        s = jnp.where(in_win, _dot_nt(qs, kb) - bias, -jnp.inf)
        s = jnp.where(key_pos >= 0, s, -jnp.inf)
        mx = jnp.maximum(jnp.max(s, axis=-1, keepdims=True), sink)
        p = jnp.exp(s - mx)
        den = jnp.sum(p, axis=-1, keepdims=True) + jnp.exp(sink - mx)
        o = _dot(p.astype(BF16), vb) / den
        zf = z_ref[0, rows, :].astype(F32)
        o2 = jnp.concatenate([o[:w], o[w:]], axis=1)
        o_ref[0, rows, :] = (o2 * _silu(zf)).astype(BF16)


def _swa(p3, slopes, sinks, qg, kg, *, tq=512):
    bsz, t, _ = p3.shape
    w = SWA_WINDOW
    nb = tq // w
    kern = functools.partial(_swa_kernel, tq=tq)
    smem = pl.BlockSpec(memory_space=pltpu.SMEM)
    prev = lambda base: (lambda b, g, i: (b, jnp.maximum(i * nb - 1, 0), base + g))
    return pl.pallas_call(
        kern,
        grid=(bsz, 2, t // tq),
        in_specs=[smem, smem,
                  pl.BlockSpec((1, tq, 256), lambda b, g, i: (b, i, _blk("bq", 256) + g)),
                  pl.BlockSpec((1, tq, 128), lambda b, g, i: (b, i, _blk("bk", 128) + g)),
                  pl.BlockSpec((1, w, 128), prev(_blk("bk", 128))),
                  pl.BlockSpec((1, tq, 128), lambda b, g, i: (b, i, _blk("bv", 128) + g)),
                  pl.BlockSpec((1, w, 128), prev(_blk("bv", 128))),
                  pl.BlockSpec((1, tq, 256), lambda b, g, i: (b, i, _blk("bz", 256) + g)),
                  pl.BlockSpec((1, HEAD_DIM), lambda b, g, i: (0, 0)),
                  pl.BlockSpec((1, HEAD_DIM), lambda b, g, i: (0, 0))],
        out_specs=pl.BlockSpec((1, tq, 256), lambda b, g, i: (b, i, g)),
        out_shape=jax.ShapeDtypeStruct((bsz, t, GROUP_WIDTH), BF16),
        compiler_params=pltpu.CompilerParams(dimension_semantics=("parallel", "parallel", "arbitrary"),
                                             vmem_limit_bytes=VMEM_LIMIT),
        name="swa",
    )(slopes, sinks, p3, p3, p3, p3, p3, p3, qg, kg)


DIFF_KAUG = 256
DIFF_POS_SPLIT = 256
DIFF_TK = 256


def _diff_prep_kernel(q_ref, k_ref, v_ref, qg_ref, kg_ref, qt_ref, ka_ref, vt_ref, *, blk, slopes):
    i = pl.program_id(1)
    lane = lax.broadcasted_iota(jnp.int32, (blk, HEAD_DIM), 1)
    low = lane < DIFF_DQK
    pos = i * blk + lax.broadcasted_iota(jnp.int32, (blk, HEAD_DIM), 0)
    pos_lo_i = jnp.bitwise_and(pos, DIFF_POS_SPLIT - 1)
    pos_hi = (pos - pos_lo_i).astype(F32)
    pos_lo = pos_lo_i.astype(F32)

    def halfnorm(x, gain):
        x2 = x * x
        s_lo = jnp.sum(jnp.where(low, x2, 0.0), axis=-1, keepdims=True)
        s_hi = jnp.sum(jnp.where(low, 0.0, x2), axis=-1, keepdims=True)
        ms = jnp.where(low, s_lo, s_hi) * (1.0 / DIFF_DQK)
        return x * lax.rsqrt(ms + NORM_EPS) * gain

    for h in range(N_HEADS):
        cols = slice(h * HEAD_DIM, (h + 1) * HEAD_DIM)
        qn = halfnorm(q_ref[0, :, cols].astype(F32), qg_ref[...]) * (DIFF_DQK ** -0.5)
        qt_ref[0, h] = qn.T.astype(BF16)
        kn = halfnorm(k_ref[0, :, cols].astype(F32), kg_ref[...])
        ka_ref[0, h, :, 0:HEAD_DIM] = kn.astype(BF16)
        bias = jnp.where(lane == 0, slopes[h] * pos_hi, jnp.where(lane == 1, slopes[h] * pos_lo, 0.0))
        ka_ref[0, h, :, HEAD_DIM:DIFF_KAUG] = bias.astype(BF16)
        for c in range(blk // DIFF_TK):
            vt_ref[0, h, c] = v_ref[0, c * DIFF_TK:(c + 1) * DIFF_TK, cols].astype(F32).T.astype(BF16)


def _diff_prep(p3, qg, kg, slopes, *, blk=512):
    bsz, t, _ = p3.shape
    kern = functools.partial(_diff_prep_kernel, blk=blk, slopes=slopes)
    return pl.pallas_call(
        kern,
        grid=(bsz, t // blk),
        in_specs=[pl.BlockSpec((1, blk, 512), lambda b, i: (b, i, _blk("cq", 512))),
                  pl.BlockSpec((1, blk, 512), lambda b, i: (b, i, _blk("ck", 512))),
                  pl.BlockSpec((1, blk, 512), lambda b, i: (b, i, _blk("cv", 512))),
                  pl.BlockSpec((1, HEAD_DIM), lambda b, i: (0, 0)),
                  pl.BlockSpec((1, HEAD_DIM), lambda b, i: (0, 0))],
        out_specs=[pl.BlockSpec((1, N_HEADS, HEAD_DIM, blk), lambda b, i: (b, 0, 0, i)),
                   pl.BlockSpec((1, N_HEADS, blk, DIFF_KAUG), lambda b, i: (b, 0, i, 0)),
                   pl.BlockSpec((1, N_HEADS, blk // DIFF_TK, HEAD_DIM, DIFF_TK), lambda b, i: (b, 0, i, 0, 0))],
        out_shape=[jax.ShapeDtypeStruct((bsz, N_HEADS, HEAD_DIM, t), BF16),
                   jax.ShapeDtypeStruct((bsz, N_HEADS, t, DIFF_KAUG), BF16),
                   jax.ShapeDtypeStruct((bsz, N_HEADS, t // DIFF_TK, HEAD_DIM, DIFF_TK), BF16)],
        compiler_params=pltpu.CompilerParams(dimension_semantics=("parallel", "parallel"),
                                             vmem_limit_bytes=VMEM_LIMIT),
        name="diff_prep",
    )(p3, p3, p3, qg, kg)


def _diff_flash_kernel(qt_ref, ka_ref, vt_ref, z_ref, lam_ref, sg_ref, o_ref,
                       qa_ref, acc_ref, m_ref, l_ref, *, tq, tk, lambda_init):
    qi = pl.program_id(2)
    qt = qt_ref[0, 0]
    r = lax.broadcasted_iota(jnp.int32, (HEAD_DIM, tq), 0)
    zero = jnp.zeros_like(qt)
    ones2 = jnp.where(r < 2, 1.0, 0.0).astype(BF16)
    qa_ref[0:HEAD_DIM, 0:tq] = jnp.where(r < DIFF_DQK, qt, zero)
    qa_ref[0:HEAD_DIM, tq:2 * tq] = jnp.where(r < DIFF_DQK, zero, qt)
    qa_ref[HEAD_DIM:DIFF_KAUG, 0:tq] = ones2
    qa_ref[HEAD_DIM:DIFF_KAUG, tq:2 * tq] = ones2
    m_ref[...] = jnp.full_like(m_ref, -1e30)
    l_ref[...] = jnp.zeros_like(l_ref)
    acc_ref[...] = jnp.zeros_like(acc_ref)

    def step(j, masked):
        k0 = pl.multiple_of(j * tk, tk)
        s = _dot(ka_ref[0, 0, pl.ds(k0, tk), :], qa_ref[...])
        if masked:
            key = k0 + lax.broadcasted_iota(jnp.int32, (tk, 2 * tq), 0)
            qry = qi * tq + jnp.bitwise_and(lax.broadcasted_iota(jnp.int32, (tk, 2 * tq), 1), tq - 1)
            s = jnp.where(key <= qry, s, -jnp.inf)
        m_old = m_ref[...]
        m_new = jnp.maximum(m_old, jnp.max(s, axis=0, keepdims=True))
        alpha = jnp.exp(m_old - m_new)
        p = jnp.exp(s - m_new)
        l_ref[...] = alpha * l_ref[...] + jnp.sum(p, axis=0, keepdims=True)
        m_ref[...] = m_new
        acc_ref[...] = alpha * acc_ref[...] + _dot(vt_ref[0, 0, j], p.astype(BF16))

    n_full = qi * (tq // tk)

    def body(j, carry):
        step(j, False)
        return carry

    lax.fori_loop(0, n_full, body, 0)
    for d in range(tq // tk):
        step(n_full + d, True)

    lam = lam_ref[...]
    lam_full = (jnp.exp(jnp.sum(lam[0:1] * lam[1:2], axis=-1, keepdims=True))
                - jnp.exp(jnp.sum(lam[2:3] * lam[3:4], axis=-1, keepdims=True)) + lambda_init)
    o_all = acc_ref[...] / l_ref[...]
    o = o_all[:, 0:tq] - lam_full * o_all[:, tq:2 * tq]
    var = jnp.mean(o * o, axis=0, keepdims=True)
    on = o * lax.rsqrt(var + NORM_EPS) * sg_ref[...] * (1.0 - lambda_init)
    zf = z_ref[0].astype(F32)
    o_ref[0] = (on.T * _silu(zf)).astype(BF16)


def _diff_flash(qt, ka, vt, p3, lam, sg_col, lambda_init, *, tq=256):
    bsz, _, _, t = qt.shape
    tk = DIFF_TK
    assert tq % tk == 0 and tq & (tq - 1) == 0
    kern = functools.partial(_diff_flash_kernel, tq=tq, tk=tk, lambda_init=lambda_init)
    return pl.pallas_call(
        kern,
        grid=(bsz, N_HEADS, t // tq),
        in_specs=[pl.BlockSpec((1, 1, HEAD_DIM, tq), lambda b, h, i: (b, h, 0, i)),
                  pl.BlockSpec((1, 1, t, DIFF_KAUG), lambda b, h, i: (b, h, 0, 0)),
                  pl.BlockSpec((1, 1, t // tk, HEAD_DIM, tk), lambda b, h, i: (b, h, 0, 0, 0)),
                  pl.BlockSpec((1, tq, HEAD_DIM), lambda b, h, i: (b, i, _blk("cz", 128) + h)),
                  pl.BlockSpec((4, DIFF_DQK), lambda b, h, i: (0, 0)),
                  pl.BlockSpec((HEAD_DIM, 1), lambda b, h, i: (0, 0))],
        out_specs=pl.BlockSpec((1, tq, HEAD_DIM), lambda b, h, i: (b, i, h)),
        out_shape=jax.ShapeDtypeStruct((bsz, t, GROUP_WIDTH), BF16),
        scratch_shapes=[pltpu.VMEM((DIFF_KAUG, 2 * tq), BF16),
                        pltpu.VMEM((HEAD_DIM, 2 * tq), F32),
                        pltpu.VMEM((1, 2 * tq), F32),
                        pltpu.VMEM((1, 2 * tq), F32)],
        compiler_params=pltpu.CompilerParams(dimension_semantics=("parallel", "parallel", "arbitrary"),
                                             vmem_limit_bytes=VMEM_LIMIT),
        name="diff_flash",
    )(qt, ka, vt, p3, lam, sg_col)


GDN_TAIL = 16


def _gdn_kernel(qkv_ref, z_ref, ps_ref, cw_ref, alog_ref, dtb_ref, ng_ref, o_ref,
                xe_ref, s_ref, qn_ref, kn_ref, vv_ref, gate_ref, beta_ref, *, blk):
    @pl.when(pl.program_id(1) == 0)
    def _():
        s_ref[...] = jnp.zeros_like(s_ref)
        xe_ref[0:GDN_TAIL, :] = jnp.zeros((GDN_TAIL, 3 * GROUP_WIDTH), F32)

    xe_ref[GDN_TAIL:GDN_TAIL + blk, :] = qkv_ref[0].astype(F32)
    cw = cw_ref[...]
    y = cw[0:1, :] * xe_ref[GDN_TAIL - 3:GDN_TAIL - 3 + blk, :]
    for j in range(1, GDN_CONV):
        y = y + cw[j:j + 1, :] * xe_ref[GDN_TAIL - 3 + j:GDN_TAIL - 3 + j + blk, :]
    xe_ref[0:GDN_TAIL, :] = xe_ref[blk:blk + GDN_TAIL, :]
    y = _silu(y)

    def l2n(x):
        return x * lax.rsqrt(jnp.sum(x * x, axis=-1, keepdims=True) + NORM_EPS)

    for h in range(N_HEADS):
        cols = slice(h * HEAD_DIM, (h + 1) * HEAD_DIM)
        qn_ref[:, cols] = l2n(y[:, h * HEAD_DIM:(h + 1) * HEAD_DIM]) * (HEAD_DIM ** -0.5)
        kn_ref[:, cols] = l2n(y[:, GROUP_WIDTH + h * HEAD_DIM:GROUP_WIDTH + (h + 1) * HEAD_DIM])
    vv_ref[...] = y[:, 2 * GROUP_WIDTH:3 * GROUP_WIDTH]

    ps = ps_ref[0]
    gate_ref[...] = -jnp.exp(alog_ref[...]) * _softplus(ps + dtb_ref[...])
    beta_ref[...] = 1.0 / (1.0 + jnp.exp(-ps))

    tril, strict = _tri_masks(CHUNK)
    tri = jnp.where(tril, 1.0, 0.0).astype(BF16)
    eye = jnp.where(tril, 1.0, 0.0) - jnp.where(strict, 1.0, 0.0)
    ng = ng_ref[...]

    def chunk(c, carry):
        r0 = pl.multiple_of(c * CHUNK, CHUNK)
        rows = pl.ds(r0, CHUNK)
        dec = _cumsum_rows(tri, gate_ref[rows, :])
        dec_t = dec.T
        e_dec = jnp.exp(dec)
        e_end = jnp.exp(dec[CHUNK - 1:CHUNK, :] - dec)
        beta = beta_ref[rows, :]
        for h in range(N_HEADS):
            cols = slice(h * HEAD_DIM, (h + 1) * HEAD_DIM)
            la = SMALL_A + h
            dcol = dec[:, la:la + 1]
            drow = dec_t[la:la + 1, :]
            diff = dcol - drow
            dmask = jnp.where(tril, jnp.exp(jnp.where(tril, diff, 0.0)), 0.0)
            ecol = e_dec[:, la:la + 1]
            eend = e_end[:, la:la + 1]
            d_last = e_dec[CHUNK - 1:CHUNK, la:la + 1]
            bcol = beta[:, SMALL_BETA + h:SMALL_BETA + h + 1]
            kn = kn_ref[rows, cols]
            qn = qn_ref[rows, cols]
            vv = vv_ref[rows, cols]
            kb = kn * bcol
            kn_b = kn.astype(BF16)
            m = jnp.where(strict, _dot_nt(kb.astype(BF16), kn_b) * dmask, 0.0)
            qk = _dot_nt(qn.astype(BF16), kn_b) * dmask
            tm = eye - m
            pw = m
            for _ in range(5):
                pw_b = pw.astype(BF16)
                pw = _dot(pw_b, pw_b)
                tm = tm + _dot(tm.astype(BF16), pw.astype(BF16))
            tm_b = tm.astype(BF16)
            u = _dot(tm_b, (vv * bcol).astype(BF16))
            wm = _dot(tm_b, (kb * ecol).astype(BF16))
            s = s_ref[h]
            s_b = s.astype(BF16)
            v_new = u - _dot(wm.astype(BF16), s_b)
            v_new_b = v_new.astype(BF16)
            o = _dot((qn * ecol).astype(BF16), s_b) + _dot(qk.astype(BF16), v_new_b)
            s_ref[h] = s * d_last + _dot_tn((kn * eend).astype(BF16), v_new_b)
            zf = z_ref[0, rows, cols].astype(F32)
            o_ref[0, rows, cols] = (_rms(o, ng) * _silu(zf)).astype(BF16)
        return carry

    lax.fori_loop(0, blk // CHUNK, chunk, 0)


def _gdn(p3, ps3, cw, alog_row, dtb_row, ng, *, blk=512):
    bsz, t, _ = p3.shape
    kern = functools.partial(_gdn_kernel, blk=blk)
    return pl.pallas_call(
        kern,
        grid=(bsz, t // blk),
        in_specs=[pl.BlockSpec((1, blk, 1536), lambda b, i: (b, i, _blk("dqkv", 1536))),
                  pl.BlockSpec((1, blk, 512), lambda b, i: (b, i, _blk("dz", 512))),
                  pl.BlockSpec((1, blk, LANES), lambda b, i: (b, i, 0)),
                  pl.BlockSpec((GDN_CONV, 1536), lambda b, i: (0, 0)),
                  pl.BlockSpec((1, LANES), lambda b, i: (0, 0)),
                  pl.BlockSpec((1, LANES), lambda b, i: (0, 0)),
                  pl.BlockSpec((1, HEAD_DIM), lambda b, i: (0, 0))],
        out_specs=pl.BlockSpec((1, blk, GROUP_WIDTH), lambda b, i: (b, i, 0)),
        out_shape=jax.ShapeDtypeStruct((bsz, t, GROUP_WIDTH), BF16),
        scratch_shapes=[pltpu.VMEM((GDN_TAIL + blk, 3 * GROUP_WIDTH), F32),
                        pltpu.VMEM((N_HEADS, HEAD_DIM, HEAD_DIM), F32),
                        pltpu.VMEM((blk, GROUP_WIDTH), F32),
                        pltpu.VMEM((blk, GROUP_WIDTH), F32),
                        pltpu.VMEM((blk, GROUP_WIDTH), F32),
                        pltpu.VMEM((blk, LANES), F32),
                        pltpu.VMEM((blk, LANES), F32)],
        compiler_params=pltpu.CompilerParams(dimension_semantics=("parallel", "arbitrary"),
                                             vmem_limit_bytes=VMEM_LIMIT),
        name="gdn",
    )(p3, p3, ps3, cw, alog_row, dtb_row, ng)


def _alibi_slopes():
    n = 2 * N_HEADS
    s = [2.0 ** (-8.0 * (i + 1) / n) for i in range(n)]
    return s[0::2], s[1::2]


def _pack_w_in(w_in):
    cols = [w_in[..., _REF_OFF[n][0]:_REF_OFF[n][0] + _REF_OFF[n][1]] for n in _MAIN_ORDER]
    w_main = jnp.concatenate(cols, axis=-1).astype(BF16)
    small = [w_in[..., _REF_OFF[n][0]:_REF_OFF[n][0] + _REF_OFF[n][1]] for n in ("alr", "dbeta", "da")]
    w_small = jnp.concatenate(small, axis=-1)
    w_small = jnp.pad(w_small, ((0, 0), (0, 0), (0, LANES - w_small.shape[-1]))).astype(BF16)
    return w_main, w_small


def _lane_row(vals, start):
    depth, n = vals.shape
    return jnp.pad(vals.astype(F32), ((0, 0), (start, LANES - start - n)))[:, None, :]


def kernel(x, norm_g, w_in, w_out, gla_w_lr2, gla_b_lr2, gla_norm_g, swa_q_norm_g, swa_k_norm_g, swa_sinks,
           diff_q_norm_g, diff_k_norm_g, diff_lambda, diff_subln_g, gdn_conv_w, gdn_A_log, gdn_dt_bias,
           gdn_norm_g):
    bsz, t, d = x.shape
    depth = w_in.shape[0]
    slopes_b, slopes_c = _alibi_slopes()
    w_main, w_small = _pack_w_in(w_in)
    w_out_b = w_out.astype(BF16)
    wlr = jnp.pad(gla_w_lr2, ((0, 0), (0, LANES - GLA_RANK), (0, 0))).astype(BF16)
    alog_row = _lane_row(gdn_A_log, SMALL_A)
    dtb_row = _lane_row(gdn_dt_bias, SMALL_A)
    slopes_b_arr = jnp.asarray(slopes_b, F32)

    x2 = x.reshape(bsz * t, d)
    for l in range(depth):
        lambda_init = 0.8 - 0.6 * math.exp(-0.3 * l)
        p_main, p_small = _inproj(x2, norm_g[l][None, :], w_main[l], w_small[l])
        p3 = p_main.reshape(bsz, t, MAIN_WIDTH)
        ps3 = p_small.reshape(bsz, t, LANES)
        ya = _gla(p3, ps3, wlr[l], gla_b_lr2[l][None, :], gla_norm_g[l][None, :])
        yb = _swa(p3, slopes_b_arr, swa_sinks[l], swa_q_norm_g[l][None, :], swa_k_norm_g[l][None, :])
        qt, ka, vt = _diff_prep(p3, jnp.tile(diff_q_norm_g[l], 2)[None, :],
                                jnp.tile(diff_k_norm_g[l], 2)[None, :], tuple(slopes_c))
        yc = _diff_flash(qt, ka, vt, p3, diff_lambda[l], diff_subln_g[l][:, None], lambda_init)
        yd = _gdn(p3, ps3, gdn_conv_w[l], alog_row[l], dtb_row[l], gdn_norm_g[l][None, :])
        ys = [y.reshape(bsz * t, GROUP_WIDTH) for y in (ya, yb, yc, yd)]
        x2 = _outproj(ys, x2, w_out_b[l])
    return x2.reshape(bsz, t, d)
```

```python
import functools
import math

import numpy as np
import jax
import jax.numpy as jnp
from jax import lax
from jax.experimental import pallas as pl
from jax.experimental.pallas import tpu as pltpu

F32 = jnp.float32
BF16 = jnp.bfloat16

D_MODEL = 2048
N_HEADS = 4
HEAD_DIM = 128
GROUP_WIDTH = 512
GLA_DK = 64
GLA_RANK = 16
GLA_GATE_NORM = 16.0
CHUNK = 64
SWA_WINDOW = 128
DIFF_DQK = 64
GDN_CONV = 4
NORM_EPS = 1e-6
LANES = 128
VMEM_LIMIT = 56 * 1024 * 1024

_REF_SIZES = (("aq", 256), ("ak", 256), ("av", 512), ("alr", 16), ("az", 512),
              ("bq", 512), ("bk", 256), ("bv", 256), ("bz", 512),
              ("cq", 512), ("ck", 512), ("cv", 512), ("cz", 512),
              ("dqkv", 1536), ("dbeta", 4), ("da", 4), ("dz", 512))
_REF_OFF = {}
_o = 0
for _n, _w in _REF_SIZES:
    _REF_OFF[_n] = (_o, _w)
    _o += _w
PROJ_WIDTH = _o

_MAIN_ORDER = ("dqkv", "dz", "aq", "ak", "av", "az", "bq", "bk", "bv", "bz", "cq", "ck", "cv", "cz")
_MAIN_OFF = {}
_o = 0
for _n in _MAIN_ORDER:
    _MAIN_OFF[_n] = _o
    _o += _REF_OFF[_n][1]
MAIN_WIDTH = _o
SMALL_LR, SMALL_BETA, SMALL_A = 0, 16, 20


def _blk(name, width):
    off = _MAIN_OFF[name]
    assert off % width == 0
    return off // width


def _silu(x):
    return x / (1.0 + jnp.exp(-x))


def _softplus(x):
    return jnp.maximum(x, 0.0) + jnp.log(1.0 + jnp.exp(-jnp.abs(x)))


def _rms(x, gain):
    return x * lax.rsqrt(jnp.mean(x * x, axis=-1, keepdims=True) + NORM_EPS) * gain


def _dot(a, b):
    return jnp.dot(a, b, preferred_element_type=F32)


def _dot_nt(a, b):
    return lax.dot_general(a, b, (((1,), (1,)), ((), ())), preferred_element_type=F32)


def _dot_tn(a, b):
    return lax.dot_general(a, b, (((0,), (0,)), ((), ())), preferred_element_type=F32)


def _split3(x):
    hi = x.astype(BF16)
    r1 = x - hi.astype(F32)
    mid = r1.astype(BF16)
    lo = (r1 - mid.astype(F32)).astype(BF16)
    return hi, mid, lo


def _cumsum_rows(tri, x):
    hi, mid, lo = _split3(x)
    return _dot(tri, hi) + _dot(tri, mid) + _dot(tri, lo)


def _tri_masks(n):
    r = lax.broadcasted_iota(jnp.int32, (n, n), 0)
    c = lax.broadcasted_iota(jnp.int32, (n, n), 1)
    return r >= c, r > c


def _inproj_kernel(x_ref, g_ref, w_ref, ws_ref, p_ref, ps_ref, h_ref):
    @pl.when(pl.program_id(1) == 0)
    def _():
        x = x_ref[...]
        h = _rms(x, g_ref[...]).astype(BF16)
        h_ref[...] = h
        ps_ref[...] = _dot(h, ws_ref[...])

    p_ref[...] = _dot(h_ref[...], w_ref[...]).astype(BF16)


def _inproj(x2, gain, w_main, w_small, *, tm=1024, tn=1024):
    m, d = x2.shape
    n = w_main.shape[1]
    return pl.pallas_call(
        _inproj_kernel,
        grid=(m // tm, n // tn),
        in_specs=[pl.BlockSpec((tm, d), lambda i, j: (i, 0)),
                  pl.BlockSpec((1, d), lambda i, j: (0, 0)),
                  pl.BlockSpec((d, tn), lambda i, j: (0, j)),
                  pl.BlockSpec((d, LANES), lambda i, j: (0, 0))],
        out_specs=[pl.BlockSpec((tm, tn), lambda i, j: (i, j)),
                   pl.BlockSpec((tm, LANES), lambda i, j: (i, 0))],
        out_shape=[jax.ShapeDtypeStruct((m, n), BF16), jax.ShapeDtypeStruct((m, LANES), F32)],
        scratch_shapes=[pltpu.VMEM((tm, d), BF16)],
        compiler_params=pltpu.CompilerParams(dimension_semantics=("parallel", "arbitrary"),
                                             vmem_limit_bytes=VMEM_LIMIT),
        name="inproj",
    )(x2, gain, w_main, w_small)


def _outproj_kernel(ya_ref, yb_ref, yc_ref, yd_ref, x_ref, w_ref, o_ref):
    acc = x_ref[...]
    for g, y_ref in enumerate((ya_ref, yb_ref, yc_ref, yd_ref)):
        acc = acc + _dot(y_ref[...], w_ref[g * GROUP_WIDTH:(g + 1) * GROUP_WIDTH, :])
    o_ref[...] = acc


def _outproj(ys, x2, w_out, *, tm=512, tn=D_MODEL):
    m, d = x2.shape
    yspec = pl.BlockSpec((tm, GROUP_WIDTH), lambda i, j: (i, 0))
    return pl.pallas_call(
        _outproj_kernel,
        grid=(m // tm, d // tn),
        in_specs=[yspec, yspec, yspec, yspec,
                  pl.BlockSpec((tm, tn), lambda i, j: (i, j)),
                  pl.BlockSpec((w_out.shape[0], tn), lambda i, j: (0, j))],
        out_specs=pl.BlockSpec((tm, tn), lambda i, j: (i, j)),
        out_shape=jax.ShapeDtypeStruct((m, d), F32),
        compiler_params=pltpu.CompilerParams(dimension_semantics=("parallel", "arbitrary"),
                                             vmem_limit_bytes=VMEM_LIMIT),
        name="outproj",
    )(*ys, x2, w_out)


def _gla_kernel(q_ref, k_ref, v_ref, z_ref, ps_ref, wlr_ref, blr_ref, ng_ref, o_ref, st_ref, g_ref,
                *, blk):
    @pl.when(pl.program_id(1) == 0)
    def _():
        st_ref[...] = jnp.zeros_like(st_ref)

    logits = _dot(ps_ref[0].astype(BF16), wlr_ref[...]) + blr_ref[...]
    g_ref[...] = -_softplus(-logits) * (1.0 / GLA_GATE_NORM)

    tril, _ = _tri_masks(CHUNK)
    tri = jnp.where(tril, 1.0, 0.0).astype(BF16)
    ng = ng_ref[...]

    def chunk(c, carry):
        r0 = pl.multiple_of(c * CHUNK, CHUNK)
        rows = pl.ds(r0, CHUNK)
        b = _cumsum_rows(tri, g_ref[rows, :])
        b_last = b[CHUNK - 1:CHUNK, :]
        qf = q_ref[0, rows, :].astype(F32)
        kf = k_ref[0, rows, :].astype(F32)
        q_dec = qf * (GLA_DK ** -0.5) * jnp.exp(b)
        k_inv = kf * jnp.exp(-b)
        k_end = kf * jnp.exp(b_last - b)
        decay = jnp.exp(b_last)
        for h in range(N_HEADS):
            ks = slice(h * GLA_DK, (h + 1) * GLA_DK)
            vs = slice(h * HEAD_DIM, (h + 1) * HEAD_DIM)
            qd_h = q_dec[:, ks].astype(BF16)
            ki_h = k_inv[:, ks].astype(BF16)
            ke_h = k_end[:, ks].astype(BF16)
            v_h = v_ref[0, rows, vs]
            a = jnp.where(tril, _dot_nt(qd_h, ki_h), 0.0).astype(BF16)
            st = st_ref[h]
            o = _dot(a, v_h) + _dot_nt(qd_h, st.astype(BF16))
            st_ref[h] = st * decay[:, ks] + _dot_tn(v_h, ke_h)
            zf = z_ref[0, rows, vs].astype(F32)
            o_ref[0, rows, vs] = (_rms(o, ng) * _silu(zf)).astype(BF16)
        return carry

    lax.fori_loop(0, blk // CHUNK, chunk, 0)


def _gla(p3, ps3, wlr, blr, ng, *, blk=512):
    bsz, t, _ = p3.shape
    kern = functools.partial(_gla_kernel, blk=blk)
    return pl.pallas_call(
        kern,
        grid=(bsz, t // blk),
        in_specs=[pl.BlockSpec((1, blk, 256), lambda b, i: (b, i, _blk("aq", 256))),
                  pl.BlockSpec((1, blk, 256), lambda b, i: (b, i, _blk("ak", 256))),
                  pl.BlockSpec((1, blk, 512), lambda b, i: (b, i, _blk("av", 512))),
                  pl.BlockSpec((1, blk, 512), lambda b, i: (b, i, _blk("az", 512))),
                  pl.BlockSpec((1, blk, LANES), lambda b, i: (b, i, 0)),
                  pl.BlockSpec((LANES, 256), lambda b, i: (0, 0)),
                  pl.BlockSpec((1, 256), lambda b, i: (0, 0)),
                  pl.BlockSpec((1, HEAD_DIM), lambda b, i: (0, 0))],
        out_specs=pl.BlockSpec((1, blk, GROUP_WIDTH), lambda b, i: (b, i, 0)),
        out_shape=jax.ShapeDtypeStruct((bsz, t, GROUP_WIDTH), BF16),
        scratch_shapes=[pltpu.VMEM((N_HEADS, HEAD_DIM, GLA_DK), F32), pltpu.VMEM((blk, 256), F32)],
        compiler_params=pltpu.CompilerParams(dimension_semantics=("parallel", "arbitrary"),
                                             vmem_limit_bytes=VMEM_LIMIT),
        name="gla",
    )(p3, p3, p3, p3, ps3, wlr, blr, ng)


def _swa_kernel(slopes_ref, sinks_ref, q_ref, k_ref, kp_ref, v_ref, vp_ref, z_ref, qg_ref, kg_ref,
                o_ref, *, tq):
    w = SWA_WINDOW
    g = pl.program_id(1)
    i = pl.program_id(2)
    kc = _rms(k_ref[0].astype(F32), kg_ref[...]).astype(BF16)
    kp = _rms(kp_ref[0].astype(F32), kg_ref[...]).astype(BF16)
    row = lax.broadcasted_iota(jnp.int32, (2 * w, 1), 0)
    first = row < w
    slope = jnp.where(first, slopes_ref[2 * g], slopes_ref[2 * g + 1])
    sink = jnp.where(first, sinks_ref[2 * g], sinks_ref[2 * g + 1])
    qi = jnp.bitwise_and(lax.broadcasted_iota(jnp.int32, (2 * w, 2 * w), 0), w - 1)
    kj = lax.broadcasted_iota(jnp.int32, (2 * w, 2 * w), 1)
    dist = qi + w - kj
    in_win = lax.bitcast_convert_type(dist, jnp.uint32) < w
    bias = slope * dist.astype(F32)
    for n in range(tq // w):
        rows = slice(n * w, (n + 1) * w)
        k_prev = kp if n == 0 else kc[(n - 1) * w:n * w]
        v_prev = vp_ref[0] if n == 0 else v_ref[0, (n - 1) * w:n * w, :]
        kb = jnp.concatenate([k_prev, kc[rows]], axis=0)
        vb = jnp.concatenate([v_prev, v_ref[0, rows, :]], axis=0)
        qf = q_ref[0, rows, :].astype(F32)
        qs = jnp.concatenate([qf[:, :HEAD_DIM], qf[:, HEAD_DIM:]], axis=0)
        qs = (_rms(qs, qg_ref[...]) * (HEAD_DIM ** -0.5)).astype(BF16)
        key_pos = (i * (tq // w) + n - 1) * w + kj
        s = jnp.where(in_win, _dot_nt(qs, kb) - bias, -jnp.inf)
        s = jnp.where(key_pos >= 0, s, -jnp.inf)
        mx = jnp.maximum(jnp.max(s, axis=-1, keepdims=True), sink)
        p = jnp.exp(s - mx)
        den = jnp.sum(p, axis=-1, keepdims=True) + jnp.exp(sink - mx)
        o = _dot(p.astype(BF16), vb) / den
        zf = z_ref[0, rows, :].astype(F32)
        o2 = jnp.concatenate([o[:w], o[w:]], axis=1)
        o_ref[0, rows, :] = (o2 * _silu(zf)).astype(BF16)


def _swa(p3, slopes, sinks, qg, kg, *, tq=512):
    bsz, t, _ = p3.shape
    w = SWA_WINDOW
    nb = tq // w
    kern = functools.partial(_swa_kernel, tq=tq)
    smem = pl.BlockSpec(memory_space=pltpu.SMEM)
    prev = lambda base: (lambda b, g, i: (b, jnp.maximum(i * nb - 1, 0), base + g))
    return pl.pallas_call(
        kern,
        grid=(bsz, 2, t // tq),
        in_specs=[smem, smem,
                  pl.BlockSpec((1, tq, 256), lambda b, g, i: (b, i, _blk("bq", 256) + g)),
                  pl.BlockSpec((1, tq, 128), lambda b, g, i: (b, i, _blk("bk", 128) + g)),
                  pl.BlockSpec((1, w, 128), prev(_blk("bk", 128))),
                  pl.BlockSpec((1, tq, 128), lambda b, g, i: (b, i, _blk("bv", 128) + g)),
                  pl.BlockSpec((1, w, 128), prev(_blk("bv", 128))),
                  pl.BlockSpec((1, tq, 256), lambda b, g, i: (b, i, _blk("bz", 256) + g)),
                  pl.BlockSpec((1, HEAD_DIM), lambda b, g, i: (0, 0)),
                  pl.BlockSpec((1, HEAD_DIM), lambda b, g, i: (0, 0))],
        out_specs=pl.BlockSpec((1, tq, 256), lambda b, g, i: (b, i, g)),
        out_shape=jax.ShapeDtypeStruct((bsz, t, GROUP_WIDTH), BF16),
        compiler_params=pltpu.CompilerParams(dimension_semantics=("parallel", "parallel", "arbitrary"),
                                             vmem_limit_bytes=VMEM_LIMIT),
        name="swa",
    )(slopes, sinks, p3, p3, p3, p3, p3, p3, qg, kg)


DIFF_KAUG = 256
DIFF_TK = 512
DIFF_VROWS = 144
LOG2E = 1.4426950408889634


def _diff_prep_kernel(q_ref, k_ref, v_ref, qg_ref, kg_ref, qt_ref, ka_ref, vt_ref, *, blk, slopes):
    i = pl.program_id(1)
    lane = lax.broadcasted_iota(jnp.int32, (blk, HEAD_DIM), 1)
    low = lane < DIFF_DQK
    pos = (i * blk + lax.broadcasted_iota(jnp.int32, (blk, HEAD_DIM), 0)).astype(F32)
    vrow = lax.broadcasted_iota(jnp.int32, (DIFF_VROWS - HEAD_DIM, DIFF_TK), 0)
    ones_rows = jnp.where(vrow == 0, 1.0, 0.0).astype(BF16)

    def halfnorm(x, gain):
        x2 = x * x
        s_lo = jnp.sum(jnp.where(low, x2, 0.0), axis=-1, keepdims=True)
        s_hi = jnp.sum(jnp.where(low, 0.0, x2), axis=-1, keepdims=True)
        ms = jnp.where(low, s_lo, s_hi) * (1.0 / DIFF_DQK)
        return x * lax.rsqrt(ms + NORM_EPS) * gain

    for h in range(N_HEADS):
        cols = slice(h * HEAD_DIM, (h + 1) * HEAD_DIM)
        qn = halfnorm(q_ref[0, :, cols].astype(F32), qg_ref[...]) * (DIFF_DQK ** -0.5 * LOG2E)
        qt_ref[0, h] = qn.T.astype(BF16)
        kn = halfnorm(k_ref[0, :, cols].astype(F32), kg_ref[...])
        ka_ref[0, h, :, 0:HEAD_DIM] = kn.astype(BF16)
        b_hi, b_mid, b_lo = (b.astype(F32) for b in _split3(pos * (slopes[h] * LOG2E)))
        bias = jnp.where(lane == 0, b_hi, jnp.where(lane == 1, b_mid, jnp.where(lane == 2, b_lo, 0.0)))
        ka_ref[0, h, :, HEAD_DIM:DIFF_KAUG] = bias.astype(BF16)
        for c in range(blk // DIFF_TK):
            vt_ref[0, h, c, 0:HEAD_DIM, :] = (
                v_ref[0, c * DIFF_TK:(c + 1) * DIFF_TK, cols].astype(F32).T.astype(BF16))
            vt_ref[0, h, c, HEAD_DIM:DIFF_VROWS, :] = ones_rows


def _diff_prep(p3, qg, kg, slopes, *, blk=512):
    bsz, t, _ = p3.shape
    kern = functools.partial(_diff_prep_kernel, blk=blk, slopes=slopes)
    return pl.pallas_call(
        kern,
        grid=(bsz, t // blk),
        in_specs=[pl.BlockSpec((1, blk, 512), lambda b, i: (b, i, _blk("cq", 512))),
                  pl.BlockSpec((1, blk, 512), lambda b, i: (b, i, _blk("ck", 512))),
                  pl.BlockSpec((1, blk, 512), lambda b, i: (b, i, _blk("cv", 512))),
                  pl.BlockSpec((1, HEAD_DIM), lambda b, i: (0, 0)),
                  pl.BlockSpec((1, HEAD_DIM), lambda b, i: (0, 0))],
        out_specs=[pl.BlockSpec((1, N_HEADS, HEAD_DIM, blk), lambda b, i: (b, 0, 0, i)),
                   pl.BlockSpec((1, N_HEADS, blk, DIFF_KAUG), lambda b, i: (b, 0, i, 0)),
                   pl.BlockSpec((1, N_HEADS, blk // DIFF_TK, DIFF_VROWS, DIFF_TK),
                                lambda b, i: (b, 0, i, 0, 0))],
        out_shape=[jax.ShapeDtypeStruct((bsz, N_HEADS, HEAD_DIM, t), BF16),
                   jax.ShapeDtypeStruct((bsz, N_HEADS, t, DIFF_KAUG), BF16),
                   jax.ShapeDtypeStruct((bsz, N_HEADS, t // DIFF_TK, DIFF_VROWS, DIFF_TK), BF16)],
        compiler_params=pltpu.CompilerParams(dimension_semantics=("parallel", "parallel"),
                                             vmem_limit_bytes=VMEM_LIMIT),
        name="diff_prep",
    )(p3, p3, p3, qg, kg)


def _diff_flash_kernel(qt_ref, ka_ref, vt_ref, z_ref, lam_ref, sg_ref, o_ref,
                       qa_ref, acc_ref, m_ref, sa_ref, sb_ref, ma_ref, mb_ref, *, tq, tk, lambda_init):
    qi = pl.program_id(2)
    qt = qt_ref[0, 0]
    r = lax.broadcasted_iota(jnp.int32, (HEAD_DIM, tq), 0)
    zero = jnp.zeros_like(qt)
    ones3 = jnp.where(r < 3, 1.0, 0.0).astype(BF16)
    qa_ref[0:HEAD_DIM, 0:tq] = jnp.where(r < DIFF_DQK, qt, zero)
    qa_ref[0:HEAD_DIM, tq:2 * tq] = jnp.where(r < DIFF_DQK, zero, qt)
    qa_ref[HEAD_DIM:DIFF_KAUG, 0:tq] = ones3
    qa_ref[HEAD_DIM:DIFF_KAUG, tq:2 * tq] = ones3
    m_ref[...] = jnp.full_like(m_ref, -1e30)
    acc_ref[...] = jnp.zeros_like(acc_ref)

    def produce(j, s_ref, mt_ref, masked=False):
        k0 = pl.multiple_of(j * tk, tk)
        s = _dot(ka_ref[0, 0, pl.ds(k0, tk), :], qa_ref[...])
        if masked:
            key = k0 + lax.broadcasted_iota(jnp.int32, (tk, 2 * tq), 0)
            qry = qi * tq + jnp.bitwise_and(lax.broadcasted_iota(jnp.int32, (tk, 2 * tq), 1), tq - 1)
            s = jnp.where(key <= qry, s, -jnp.inf)
        s_ref[...] = s
        mt_ref[...] = jnp.max(s, axis=0, keepdims=True)

    def consume(j, s_ref, mt_ref):
        m_old = m_ref[...]
        m_new = jnp.maximum(m_old, mt_ref[...])
        alpha = jnp.exp2(m_old - m_new)
        p = jnp.exp2((s_ref[...] - m_new).astype(BF16))
        m_ref[...] = m_new
        acc_ref[...] = alpha * acc_ref[...] + _dot(vt_ref[0, 0, j], p)

    n = qi
    produce(qi, sa_ref, ma_ref, masked=True)
    consume(qi, sa_ref, ma_ref)

    @pl.when(n >= 1)
    def _():
        produce(0, sa_ref, ma_ref)

    def body(jj, carry):
        a = 2 * jj
        produce(a + 1, sb_ref, mb_ref)
        consume(a, sa_ref, ma_ref)
        produce(jnp.minimum(a + 2, n - 1), sa_ref, ma_ref)
        consume(a + 1, sb_ref, mb_ref)
        return carry

    lax.fori_loop(0, n // 2, body, 0)

    @pl.when(n % 2 == 1)
    def _():
        consume(n - 1, sa_ref, ma_ref)

    lam = lam_ref[...]
    lam_full = (jnp.exp(jnp.sum(lam[0:1] * lam[1:2], axis=-1, keepdims=True))
                - jnp.exp(jnp.sum(lam[2:3] * lam[3:4], axis=-1, keepdims=True)) + lambda_init)
    o_all = acc_ref[0:HEAD_DIM, :] / acc_ref[HEAD_DIM:HEAD_DIM + 1, :]
    o = o_all[:, 0:tq] - lam_full * o_all[:, tq:2 * tq]
    var = jnp.mean(o * o, axis=0, keepdims=True)
    on = o * lax.rsqrt(var + NORM_EPS) * sg_ref[...] * (1.0 - lambda_init)
    zf = z_ref[0].astype(F32)
    o_ref[0] = (on.T * _silu(zf)).astype(BF16)


def _diff_flash(qt, ka, vt, p3, lam, sg_col, lambda_init, *, tq=512):
    bsz, _, _, t = qt.shape
    tk = DIFF_TK
    assert tq % tk == 0 and tq & (tq - 1) == 0
    kern = functools.partial(_diff_flash_kernel, tq=tq, tk=tk, lambda_init=lambda_init)
    return pl.pallas_call(
        kern,
        grid=(bsz, N_HEADS, t // tq),
        in_specs=[pl.BlockSpec((1, 1, HEAD_DIM, tq), lambda b, h, i: (b, h, 0, i)),
                  pl.BlockSpec((1, 1, t, DIFF_KAUG), lambda b, h, i: (b, h, 0, 0)),
                  pl.BlockSpec((1, 1, t // tk, DIFF_VROWS, tk), lambda b, h, i: (b, h, 0, 0, 0)),
                  pl.BlockSpec((1, tq, HEAD_DIM), lambda b, h, i: (b, i, _blk("cz", 128) + h)),
                  pl.BlockSpec((4, DIFF_DQK), lambda b, h, i: (0, 0)),
                  pl.BlockSpec((HEAD_DIM, 1), lambda b, h, i: (0, 0))],
        out_specs=pl.BlockSpec((1, tq, HEAD_DIM), lambda b, h, i: (b, i, h)),
        out_shape=jax.ShapeDtypeStruct((bsz, t, GROUP_WIDTH), BF16),
        scratch_shapes=[pltpu.VMEM((DIFF_KAUG, 2 * tq), BF16),
                        pltpu.VMEM((DIFF_VROWS, 2 * tq), F32),
                        pltpu.VMEM((1, 2 * tq), F32),
                        pltpu.VMEM((tk, 2 * tq), F32),
                        pltpu.VMEM((tk, 2 * tq), F32),
                        pltpu.VMEM((1, 2 * tq), F32),
                        pltpu.VMEM((1, 2 * tq), F32)],
        compiler_params=pltpu.CompilerParams(dimension_semantics=("parallel", "parallel", "arbitrary"),
                                             vmem_limit_bytes=VMEM_LIMIT),
        name="diff_flash",
    )(qt, ka, vt, p3, lam, sg_col)


GDN_TAIL = 16
GDN_GROUP = 256
assert CHUNK == 64 and GDN_GROUP % (2 * CHUNK) == 0


def _gdn_kernel(qkv_ref, z_ref, ps_ref, cw_ref, alog_ref, dtb_ref, ng_ref, o_ref,
                xe_ref, s_ref, qn_ref, kn_ref, vv_ref, gate_ref, beta_ref, *, blk):
    @pl.when(pl.program_id(1) == 0)
    def _():
        s_ref[...] = jnp.zeros_like(s_ref)
        xe_ref[0:GDN_TAIL, :] = jnp.zeros((GDN_TAIL, 3 * GROUP_WIDTH), F32)

    xe_ref[GDN_TAIL:GDN_TAIL + blk, :] = qkv_ref[0].astype(F32)
    cw = cw_ref[...]
    y = cw[0:1, :] * xe_ref[GDN_TAIL - 3:GDN_TAIL - 3 + blk, :]
    for j in range(1, GDN_CONV):
        y = y + cw[j:j + 1, :] * xe_ref[GDN_TAIL - 3 + j:GDN_TAIL - 3 + j + blk, :]
    xe_ref[0:GDN_TAIL, :] = xe_ref[blk:blk + GDN_TAIL, :]
    y = _silu(y)

    def l2n(x):
        return x * lax.rsqrt(jnp.sum(x * x, axis=-1, keepdims=True) + NORM_EPS)

    for h in range(N_HEADS):
        cols = slice(h * HEAD_DIM, (h + 1) * HEAD_DIM)
        qn_ref[:, cols] = l2n(y[:, h * HEAD_DIM:(h + 1) * HEAD_DIM]) * (HEAD_DIM ** -0.5)
        kn_ref[:, cols] = l2n(y[:, GROUP_WIDTH + h * HEAD_DIM:GROUP_WIDTH + (h + 1) * HEAD_DIM])
    vv_ref[...] = y[:, 2 * GROUP_WIDTH:3 * GROUP_WIDTH]

    ps = ps_ref[0]
    gate_ref[...] = -jnp.exp(alog_ref[...]) * _softplus(ps + dtb_ref[...])
    beta_ref[...] = 1.0 / (1.0 + jnp.exp(-ps))

    gr = GDN_GROUP
    cpg = gr // CHUNK
    r = lax.broadcasted_iota(jnp.int32, (gr, gr), 0)
    c = lax.broadcasted_iota(jnp.int32, (gr, gr), 1)
    u32 = functools.partial(lax.bitcast_convert_type, new_dtype=jnp.uint32)
    r_in = jnp.bitwise_and(r, CHUNK - 1)
    same = jnp.right_shift(r, 6) == jnp.right_shift(c, 6)
    bd_tril = u32(r - c) <= u32(r_in)
    bd_strict = u32(r - c - 1) < u32(r_in)
    tri_bd = jnp.where(bd_tril, 1.0, 0.0).astype(BF16)
    ones_bd = jnp.where(same, 1.0, 0.0).astype(BF16)
    eye = jnp.where(r == c, 1.0, 0.0)
    ng = ng_ref[...]
    zeros_half = jnp.zeros((CHUNK, HEAD_DIM), BF16)

    n_groups = blk // gr
    probs = []
    for gi in range(n_groups):
        rows = slice(gi * gr, (gi + 1) * gr)
        g3 = _split3(gate_ref[rows, :])
        dec = _dot(tri_bd, g3[0]) + _dot(tri_bd, g3[1]) + _dot(tri_bd, g3[2])
        dec_end = _dot(ones_bd, g3[0]) + _dot(ones_bd, g3[1]) + _dot(ones_bd, g3[2])
        dec_t = dec.T
        e_dec = jnp.exp(dec)
        e_end = jnp.exp(dec_end - dec)
        e_tot = jnp.exp(dec_end)
        beta = beta_ref[rows, :]
        for h in range(N_HEADS):
            cols = slice(h * HEAD_DIM, (h + 1) * HEAD_DIM)
            la = SMALL_A + h
            diff = dec[:, la:la + 1] - dec_t[la:la + 1, :]
            dmask = jnp.where(bd_tril, jnp.exp(jnp.where(bd_tril, diff, 0.0)), 0.0)
            ecol = e_dec[:, la:la + 1]
            bcol = beta[:, SMALL_BETA + h:SMALL_BETA + h + 1]
            kn = kn_ref[rows, cols]
            qn = qn_ref[rows, cols]
            kb = kn * bcol
            kn_b = kn.astype(BF16)
            m = jnp.where(bd_strict, _dot_nt(kb.astype(BF16), kn_b) * dmask, 0.0)
            probs.append(dict(
                rows=rows, cols=cols, h=h, m=m,
                qk_b=(_dot_nt(qn.astype(BF16), kn_b) * dmask).astype(BF16),
                rhs=jnp.concatenate([(vv_ref[rows, cols] * bcol).astype(BF16), (kb * ecol).astype(BF16)], axis=1),
                qd_b=(qn * ecol).astype(BF16),
                ket_b=(kn * e_end[:, la:la + 1]).T.astype(BF16),
                d=[e_tot[ci * CHUNK:ci * CHUNK + 1, la:la + 1] for ci in range(cpg)]))

    tms = [eye - p["m"] for p in probs]
    pws = [p["m"] for p in probs]
    for _ in range(5):
        pws = [_dot(pw.astype(BF16), pw.astype(BF16)) for pw in pws]
        tms = [tm + _dot(tm.astype(BF16), pw.astype(BF16)) for tm, pw in zip(tms, pws)]
    for p, tm in zip(probs, tms):
        uw = _dot(tm.astype(BF16), p["rhs"])
        p["u"] = uw[:, 0:HEAD_DIM]
        p["w_b"] = uw[:, HEAD_DIM:2 * HEAD_DIM].astype(BF16)

    states = [s_ref[h] for h in range(N_HEADS)]
    for gi in range(n_groups):
        gp = probs[gi * N_HEADS:(gi + 1) * N_HEADS]
        v_news = [[] for _ in gp]
        o_inters = [[] for _ in gp]
        for ci in range(cpg):
            cr = slice(ci * CHUNK, (ci + 1) * CHUNK)
            pair = (ci // 2) * 2 * CHUNK
            for h, p in enumerate(gp):
                s_b = states[h].astype(BF16)
                v_new_b = (p["u"][cr] - _dot(p["w_b"][cr], s_b)).astype(BF16)
                o_inters[h].append(_dot(p["qd_b"][cr], s_b))
                v_news[h].append(v_new_b)
                vpad = jnp.concatenate([v_new_b, zeros_half] if ci % 2 == 0 else [zeros_half, v_new_b], axis=0)
                states[h] = states[h] * p["d"][ci] + _dot(p["ket_b"][:, pair:pair + 2 * CHUNK], vpad)
        for h, p in enumerate(gp):
            o = jnp.concatenate(o_inters[h], axis=0) + _dot(p["qk_b"], jnp.concatenate(v_news[h], axis=0))
            zf = z_ref[0, p["rows"], p["cols"]].astype(F32)
            o_ref[0, p["rows"], p["cols"]] = (_rms(o, ng) * _silu(zf)).astype(BF16)
    for h in range(N_HEADS):
        s_ref[h] = states[h]


def _gdn(p3, ps3, cw, alog_row, dtb_row, ng, *, blk=512):
    bsz, t, _ = p3.shape
    kern = functools.partial(_gdn_kernel, blk=blk)
    return pl.pallas_call(
        kern,
        grid=(bsz, t // blk),
        in_specs=[pl.BlockSpec((1, blk, 1536), lambda b, i: (b, i, _blk("dqkv", 1536))),
                  pl.BlockSpec((1, blk, 512), lambda b, i: (b, i, _blk("dz", 512))),
                  pl.BlockSpec((1, blk, LANES), lambda b, i: (b, i, 0)),
                  pl.BlockSpec((GDN_CONV, 1536), lambda b, i: (0, 0)),
                  pl.BlockSpec((1, LANES), lambda b, i: (0, 0)),
                  pl.BlockSpec((1, LANES), lambda b, i: (0, 0)),
                  pl.BlockSpec((1, HEAD_DIM), lambda b, i: (0, 0))],
        out_specs=pl.BlockSpec((1, blk, GROUP_WIDTH), lambda b, i: (b, i, 0)),
        out_shape=jax.ShapeDtypeStruct((bsz, t, GROUP_WIDTH), BF16),
        scratch_shapes=[pltpu.VMEM((GDN_TAIL + blk, 3 * GROUP_WIDTH), F32),
                        pltpu.VMEM((N_HEADS, HEAD_DIM, HEAD_DIM), F32),
                        pltpu.VMEM((blk, GROUP_WIDTH), F32),
                        pltpu.VMEM((blk, GROUP_WIDTH), F32),
                        pltpu.VMEM((blk, GROUP_WIDTH), F32),
                        pltpu.VMEM((blk, LANES), F32),
                        pltpu.VMEM((blk, LANES), F32)],
        compiler_params=pltpu.CompilerParams(dimension_semantics=("parallel", "arbitrary"),
                                             vmem_limit_bytes=VMEM_LIMIT),
        name="gdn",
    )(p3, p3, ps3, cw, alog_row, dtb_row, ng)


def _alibi_slopes():
    n = 2 * N_HEADS
    s = [2.0 ** (-8.0 * (i + 1) / n) for i in range(n)]
    return s[0::2], s[1::2]


def _pack_w_in(w_in):
    cols = [w_in[..., _REF_OFF[n][0]:_REF_OFF[n][0] + _REF_OFF[n][1]] for n in _MAIN_ORDER]
    w_main = jnp.concatenate(cols, axis=-1).astype(BF16)
    small = [w_in[..., _REF_OFF[n][0]:_REF_OFF[n][0] + _REF_OFF[n][1]] for n in ("alr", "dbeta", "da")]
    w_small = jnp.concatenate(small, axis=-1)
    w_small = jnp.pad(w_small, ((0, 0), (0, 0), (0, LANES - w_small.shape[-1]))).astype(BF16)
    return w_main, w_small


def _lane_row(vals, start):
    depth, n = vals.shape
    return jnp.pad(vals.astype(F32), ((0, 0), (start, LANES - start - n)))[:, None, :]


def kernel(x, norm_g, w_in, w_out, gla_w_lr2, gla_b_lr2, gla_norm_g, swa_q_norm_g, swa_k_norm_g, swa_sinks,
           diff_q_norm_g, diff_k_norm_g, diff_lambda, diff_subln_g, gdn_conv_w, gdn_A_log, gdn_dt_bias,
           gdn_norm_g):
    bsz, t, d = x.shape
    depth = w_in.shape[0]
    slopes_b, slopes_c = _alibi_slopes()
    w_main, w_small = _pack_w_in(w_in)
    w_out_b = w_out.astype(BF16)
    wlr = jnp.pad(gla_w_lr2, ((0, 0), (0, LANES - GLA_RANK), (0, 0))).astype(BF16)
    alog_row = _lane_row(gdn_A_log, SMALL_A)
    dtb_row = _lane_row(gdn_dt_bias, SMALL_A)
    slopes_b_arr = jnp.asarray(slopes_b, F32)

    x2 = x.reshape(bsz * t, d)
    for l in range(depth):
        lambda_init = 0.8 - 0.6 * math.exp(-0.3 * l)
        p_main, p_small = _inproj(x2, norm_g[l][None, :], w_main[l], w_small[l])
        p3 = p_main.reshape(bsz, t, MAIN_WIDTH)
        ps3 = p_small.reshape(bsz, t, LANES)
        ya = _gla(p3, ps3, wlr[l], gla_b_lr2[l][None, :], gla_norm_g[l][None, :])
        yb = _swa(p3, slopes_b_arr, swa_sinks[l], swa_q_norm_g[l][None, :], swa_k_norm_g[l][None, :])
        qt, ka, vt = _diff_prep(p3, jnp.tile(diff_q_norm_g[l], 2)[None, :],
                                jnp.tile(diff_k_norm_g[l], 2)[None, :], tuple(slopes_c))
        yc = _diff_flash(qt, ka, vt, p3, diff_lambda[l], diff_subln_g[l][:, None], lambda_init)
        yd = _gdn(p3, ps3, gdn_conv_w[l], alog_row[l], dtb_row[l], gdn_norm_g[l][None, :])
        ys = [y.reshape(bsz * t, GROUP_WIDTH) for y in (ya, yb, yc, yd)]
        x2 = _outproj(ys, x2, w_out_b[l])
    return x2.reshape(bsz, t, d)
```

```python
import functools
import math

import numpy as np
import jax
import jax.numpy as jnp
from jax import lax
from jax.experimental import pallas as pl
from jax.experimental.pallas import tpu as pltpu

F32 = jnp.float32
BF16 = jnp.bfloat16

D_MODEL = 2048
N_HEADS = 4
HEAD_DIM = 128
GROUP_WIDTH = 512
GLA_DK = 64
GLA_RANK = 16
GLA_GATE_NORM = 16.0
CHUNK = 64
GROUP = 256
SWA_WINDOW = 128
DIFF_DQK = 64
GDN_CONV = 4
NORM_EPS = 1e-6
LANES = 128
MXU_COLS = 256
VMEM_LIMIT = 56 * 1024 * 1024

_REF_SIZES = (("aq", 256), ("ak", 256), ("av", 512), ("alr", 16), ("az", 512),
              ("bq", 512), ("bk", 256), ("bv", 256), ("bz", 512),
              ("cq", 512), ("ck", 512), ("cv", 512), ("cz", 512),
              ("dqkv", 1536), ("dbeta", 4), ("da", 4), ("dz", 512))
_REF_OFF = {}
_o = 0
for _n, _w in _REF_SIZES:
    _REF_OFF[_n] = (_o, _w)
    _o += _w
PROJ_WIDTH = _o

_MAIN_ORDER = ("dqkv", "dz", "aq", "ak", "av", "az", "bq", "bk", "bv", "bz", "cq", "ck", "cv", "cz")
_MAIN_OFF = {}
_o = 0
for _n in _MAIN_ORDER:
    _MAIN_OFF[_n] = _o
    _o += _REF_OFF[_n][1]
MAIN_WIDTH = _o
SMALL_LR, SMALL_BETA, SMALL_A = 0, 16, 20


def _blk(name, width):
    off = _MAIN_OFF[name]
    assert off % width == 0
    return off // width


def _silu(x):
    return x / (1.0 + jnp.exp(-x))


def _softplus(x):
    return jnp.maximum(x, 0.0) + jnp.log(1.0 + jnp.exp(-jnp.abs(x)))


def _rms(x, gain):
    return x * lax.rsqrt(jnp.mean(x * x, axis=-1, keepdims=True) + NORM_EPS) * gain


def _dot(a, b):
    return jnp.dot(a, b, preferred_element_type=F32)


def _dot_nt(a, b):
    return lax.dot_general(a, b, (((1,), (1,)), ((), ())), preferred_element_type=F32)


def _dot_tn(a, b):
    return lax.dot_general(a, b, (((0,), (0,)), ((), ())), preferred_element_type=F32)


def _split3(x):
    hi = x.astype(BF16)
    r1 = x - hi.astype(F32)
    mid = r1.astype(BF16)
    lo = (r1 - mid.astype(F32)).astype(BF16)
    return hi, mid, lo


def _cumsum_rows(tri, x):
    hi, mid, lo = _split3(x)
    return _dot(tri, hi) + _dot(tri, mid) + _dot(tri, lo)


def _tri_masks(n):
    r = lax.broadcasted_iota(jnp.int32, (n, n), 0)
    c = lax.broadcasted_iota(jnp.int32, (n, n), 1)
    return r >= c, r > c


def _chunk_block_masks(gr):
    r = lax.broadcasted_iota(jnp.int32, (gr, gr), 0)
    c = lax.broadcasted_iota(jnp.int32, (gr, gr), 1)
    u32 = functools.partial(lax.bitcast_convert_type, new_dtype=jnp.uint32)
    r_in = jnp.bitwise_and(r, CHUNK - 1)
    same = jnp.right_shift(r, CHUNK.bit_length() - 1) == jnp.right_shift(c, CHUNK.bit_length() - 1)
    bd_tril = u32(r - c) <= u32(r_in)
    bd_strict = u32(r - c - 1) < u32(r_in)
    return same, bd_tril, bd_strict


def _inproj_kernel(x_ref, g_ref, w_ref, ws_ref, p_ref, ps_ref, h_ref):
    @pl.when(pl.program_id(1) == 0)
    def _():
        x = x_ref[...]
        h = _rms(x, g_ref[...]).astype(BF16)
        h_ref[...] = h
        ps_ref[...] = _dot(h, ws_ref[...])

    p_ref[...] = _dot(h_ref[...], w_ref[...]).astype(BF16)


def _inproj(x2, gain, w_main, w_small, *, tm=1024, tn=1024):
    m, d = x2.shape
    n = w_main.shape[1]
    return pl.pallas_call(
        _inproj_kernel,
        grid=(m // tm, n // tn),
        in_specs=[pl.BlockSpec((tm, d), lambda i, j: (i, 0)),
                  pl.BlockSpec((1, d), lambda i, j: (0, 0)),
                  pl.BlockSpec((d, tn), lambda i, j: (0, j)),
                  pl.BlockSpec((d, LANES), lambda i, j: (0, 0))],
        out_specs=[pl.BlockSpec((tm, tn), lambda i, j: (i, j)),
                   pl.BlockSpec((tm, LANES), lambda i, j: (i, 0))],
        out_shape=[jax.ShapeDtypeStruct((m, n), BF16), jax.ShapeDtypeStruct((m, LANES), F32)],
        scratch_shapes=[pltpu.VMEM((tm, d), BF16)],
        compiler_params=pltpu.CompilerParams(dimension_semantics=("parallel", "arbitrary"),
                                             vmem_limit_bytes=VMEM_LIMIT),
        name="inproj",
    )(x2, gain, w_main, w_small)


def _outproj_kernel(ya_ref, yb_ref, yc_ref, yd_ref, x_ref, w_ref, o_ref):
    acc = x_ref[...]
    for g, y_ref in enumerate((ya_ref, yb_ref, yc_ref, yd_ref)):
        acc = acc + _dot(y_ref[...], w_ref[g * GROUP_WIDTH:(g + 1) * GROUP_WIDTH, :])
    o_ref[...] = acc


def _outproj(ys, x2, w_out, *, tm=512, tn=D_MODEL):
    m, d = x2.shape
    yspec = pl.BlockSpec((tm, GROUP_WIDTH), lambda i, j: (i, 0))
    return pl.pallas_call(
        _outproj_kernel,
        grid=(m // tm, d // tn),
        in_specs=[yspec, yspec, yspec, yspec,
                  pl.BlockSpec((tm, tn), lambda i, j: (i, j)),
                  pl.BlockSpec((w_out.shape[0], tn), lambda i, j: (0, j))],
        out_specs=pl.BlockSpec((tm, tn), lambda i, j: (i, j)),
        out_shape=jax.ShapeDtypeStruct((m, d), F32),
        compiler_params=pltpu.CompilerParams(dimension_semantics=("parallel", "arbitrary"),
                                             vmem_limit_bytes=VMEM_LIMIT),
        name="outproj",
    )(*ys, x2, w_out)


def _gla_kernel(q_ref, k_ref, v_ref, z_ref, ps_ref, wlr_ref, blr_ref, ng_ref, o_ref, st_ref, *, blk):
    @pl.when(pl.program_id(1) == 0)
    def _():
        st_ref[...] = jnp.zeros_like(st_ref)

    logits = _dot(ps_ref[0].astype(BF16), wlr_ref[...]) + blr_ref[...]
    gates = -_softplus(-logits) * (1.0 / GLA_GATE_NORM)

    gr = GROUP
    cpg = gr // CHUNK
    same, bd_tril, _ = _chunk_block_masks(gr)
    tri_bd = jnp.where(bd_tril, 1.0, 0.0).astype(BF16)
    ones_bd = jnp.where(same, 1.0, 0.0).astype(BF16)
    ng = ng_ref[...]
    zeros_half = jnp.zeros((CHUNK, GLA_DK), BF16)
    states = [st_ref[h] for h in range(N_HEADS)]

    probs = []
    for gi in range(blk // gr):
        rows = slice(gi * gr, (gi + 1) * gr)
        g3 = _split3(gates[rows])
        b = _dot(tri_bd, g3[0]) + _dot(tri_bd, g3[1]) + _dot(tri_bd, g3[2])
        b_end = _dot(ones_bd, g3[0]) + _dot(ones_bd, g3[1]) + _dot(ones_bd, g3[2])
        qf = q_ref[0, rows, :].astype(F32)
        kf = k_ref[0, rows, :].astype(F32)
        q_dec = qf * (GLA_DK ** -0.5) * jnp.exp(b)
        k_inv = kf * jnp.exp(-b)
        k_end = kf * jnp.exp(b_end - b)
        decay = jnp.exp(b_end)
        for h in range(N_HEADS):
            ks = slice(h * GLA_DK, (h + 1) * GLA_DK)
            vs = slice(h * HEAD_DIM, (h + 1) * HEAD_DIM)
            probs.append(dict(h=h, rows=rows, vs=vs, qd=q_dec[:, ks].astype(BF16), ki=k_inv[:, ks].astype(BF16),
                              ke=k_end[:, ks].astype(BF16), v=v_ref[0, rows, vs],
                              d=[decay[ci * CHUNK:ci * CHUNK + 1, ks] for ci in range(cpg)]))
    for p in probs:
        p["a"] = jnp.where(bd_tril, _dot_nt(p["qd"], p["ki"]), 0.0).astype(BF16)
        vt = p["v"].astype(F32).T.astype(BF16)
        p["ut"] = []
        for ci in range(cpg):
            cr = slice(ci * CHUNK, (ci + 1) * CHUNK)
            pair = (ci // 2) * 2 * CHUNK
            ke_c = p["ke"][cr]
            ke_pad = jnp.concatenate([ke_c, zeros_half] if ci % 2 == 0 else [zeros_half, ke_c], axis=0)
            p["ut"].append(_dot(vt[:, pair:pair + 2 * CHUNK], ke_pad))
    for p in probs:
        h = p["h"]
        o_inters = []
        for ci in range(cpg):
            cr = slice(ci * CHUNK, (ci + 1) * CHUNK)
            o_inters.append(_dot_nt(p["qd"][cr], states[h].astype(BF16)))
            states[h] = states[h] * p["d"][ci] + p["ut"][ci]
        p["o"] = jnp.concatenate(o_inters, axis=0)
    for p in probs:
        o = _dot(p["a"], p["v"]) + p["o"]
        zf = z_ref[0, p["rows"], p["vs"]].astype(F32)
        o_ref[0, p["rows"], p["vs"]] = (_rms(o, ng) * _silu(zf)).astype(BF16)
    for h in range(N_HEADS):
        st_ref[h] = states[h]


def _gla(p3, ps3, wlr, blr, ng, *, blk=512):
    bsz, t, _ = p3.shape
    kern = functools.partial(_gla_kernel, blk=blk)
    return pl.pallas_call(
        kern,
        grid=(bsz, t // blk),
        in_specs=[pl.BlockSpec((1, blk, 256), lambda b, i: (b, i, _blk("aq", 256))),
                  pl.BlockSpec((1, blk, 256), lambda b, i: (b, i, _blk("ak", 256))),
                  pl.BlockSpec((1, blk, 512), lambda b, i: (b, i, _blk("av", 512))),
                  pl.BlockSpec((1, blk, 512), lambda b, i: (b, i, _blk("az", 512))),
                  pl.BlockSpec((1, blk, LANES), lambda b, i: (b, i, 0)),
                  pl.BlockSpec((LANES, 256), lambda b, i: (0, 0)),
                  pl.BlockSpec((1, 256), lambda b, i: (0, 0)),
                  pl.BlockSpec((1, HEAD_DIM), lambda b, i: (0, 0))],
        out_specs=pl.BlockSpec((1, blk, GROUP_WIDTH), lambda b, i: (b, i, 0)),
        out_shape=jax.ShapeDtypeStruct((bsz, t, GROUP_WIDTH), BF16),
        scratch_shapes=[pltpu.VMEM((N_HEADS, HEAD_DIM, GLA_DK), F32)],
        compiler_params=pltpu.CompilerParams(dimension_semantics=("parallel", "arbitrary"),
                                             vmem_limit_bytes=VMEM_LIMIT),
        name="gla",
    )(p3, p3, p3, p3, ps3, wlr, blr, ng)


def _swa_kernel(slopes_ref, sinks_ref, q_ref, k_ref, kp_ref, v_ref, vp_ref, z_ref, qg_ref, kg_ref,
                o_ref, *, tq):
    w = SWA_WINDOW
    g = pl.program_id(1)
    i = pl.program_id(2)
    kc = _rms(k_ref[0].astype(F32), kg_ref[...]).astype(BF16)
    kp = _rms(kp_ref[0].astype(F32), kg_ref[...]).astype(BF16)
    row = lax.broadcasted_iota(jnp.int32, (2 * w, 1), 0)
    first = row < w
    slope = jnp.where(first, slopes_ref[2 * g], slopes_ref[2 * g + 1])
    sink = jnp.where(first, sinks_ref[2 * g], sinks_ref[2 * g + 1])
    qi = jnp.bitwise_and(lax.broadcasted_iota(jnp.int32, (2 * w, 2 * w), 0), w - 1)
    kj = lax.broadcasted_iota(jnp.int32, (2 * w, 2 * w), 1)
    dist = qi + w - kj
    in_win = lax.bitcast_convert_type(dist, jnp.uint32) < w
    bias = slope * dist.astype(F32)
    for n in range(tq // w):
        rows = slice(n * w, (n + 1) * w)
        k_prev = kp if n == 0 else kc[(n - 1) * w:n * w]
        v_prev = vp_ref[0] if n == 0 else v_ref[0, (n - 1) * w:n * w, :]
        kb = jnp.concatenate([k_prev, kc[rows]], axis=0)
        vb = jnp.concatenate([v_prev, v_ref[0, rows, :]], axis=0)
        qf = q_ref[0, rows, :].astype(F32)
        qs = jnp.concatenate([qf[:, :HEAD_DIM], qf[:, HEAD_DIM:]], axis=0)
        qs = (_rms(qs, qg_ref[...]) * (HEAD_DIM ** -0.5)).astype(BF16)
        key_pos = (i * (tq // w) + n - 1) * w + kj
        s = jnp.where(in_win, _dot_nt(qs, kb) - bias, -jnp.inf)
        s = jnp.where(key_pos >= 0, s, -jnp.inf)
        mx = jnp.maximum(jnp.max(s, axis=-1, keepdims=True), sink)
        p = jnp.exp(s - mx)
        den = jnp.sum(p, axis=-1, keepdims=True) + jnp.exp(sink - mx)
        o = _dot(p.astype(BF16), vb) / den
        zf = z_ref[0, rows, :].astype(F32)
        o2 = jnp.concatenate([o[:w], o[w:]], axis=1)
        o_ref[0, rows, :] = (o2 * _silu(zf)).astype(BF16)


def _swa(p3, slopes, sinks, qg, kg, *, tq=512):
    bsz, t, _ = p3.shape
    w = SWA_WINDOW
    nb = tq // w
    kern = functools.partial(_swa_kernel, tq=tq)
    smem = pl.BlockSpec(memory_space=pltpu.SMEM)
    prev = lambda base: (lambda b, g, i: (b, jnp.maximum(i * nb - 1, 0), base + g))
    return pl.pallas_call(
        kern,
        grid=(bsz, 2, t // tq),
        in_specs=[smem, smem,
                  pl.BlockSpec((1, tq, 256), lambda b, g, i: (b, i, _blk("bq", 256) + g)),
                  pl.BlockSpec((1, tq, 128), lambda b, g, i: (b, i, _blk("bk", 128) + g)),
                  pl.BlockSpec((1, w, 128), prev(_blk("bk", 128))),
                  pl.BlockSpec((1, tq, 128), lambda b, g, i: (b, i, _blk("bv", 128) + g)),
                  pl.BlockSpec((1, w, 128), prev(_blk("bv", 128))),
                  pl.BlockSpec((1, tq, 256), lambda b, g, i: (b, i, _blk("bz", 256) + g)),
                  pl.BlockSpec((1, HEAD_DIM), lambda b, g, i: (0, 0)),
                  pl.BlockSpec((1, HEAD_DIM), lambda b, g, i: (0, 0))],
        out_specs=pl.BlockSpec((1, tq, 256), lambda b, g, i: (b, i, g)),
        out_shape=jax.ShapeDtypeStruct((bsz, t, GROUP_WIDTH), BF16),
        compiler_params=pltpu.CompilerParams(dimension_semantics=("parallel", "parallel", "arbitrary"),
                                             vmem_limit_bytes=VMEM_LIMIT),
        name="swa",
    )(slopes, sinks, p3, p3, p3, p3, p3, p3, qg, kg)


DIFF_KAUG = 256
DIFF_TK = 512
DIFF_VROWS = 144
LOG2E = 1.4426950408889634
SKIP_TILE = 1e30


def _diff_prep_kernel(q_ref, k_ref, v_ref, qg_ref, kg_ref, qt_ref, ka_ref, vt_ref, *, blk, slopes):
    i = pl.program_id(1)
    lane = lax.broadcasted_iota(jnp.int32, (blk, HEAD_DIM), 1)
    low = lane < DIFF_DQK
    pos = (i * blk + lax.broadcasted_iota(jnp.int32, (blk, HEAD_DIM), 0)).astype(F32)
    vrow = lax.broadcasted_iota(jnp.int32, (DIFF_VROWS - HEAD_DIM, DIFF_TK), 0)
    ones_rows = jnp.where(vrow == 0, 1.0, 0.0).astype(BF16)

    def halfnorm(x, gain):
        x2 = x * x
        s_lo = jnp.sum(jnp.where(low, x2, 0.0), axis=-1, keepdims=True)
        s_hi = jnp.sum(jnp.where(low, 0.0, x2), axis=-1, keepdims=True)
        ms = jnp.where(low, s_lo, s_hi) * (1.0 / DIFF_DQK)
        return x * lax.rsqrt(ms + NORM_EPS) * gain

    for h in range(N_HEADS):
        cols = slice(h * HEAD_DIM, (h + 1) * HEAD_DIM)
        qn = halfnorm(q_ref[0, :, cols].astype(F32), qg_ref[...]) * (DIFF_DQK ** -0.5 * LOG2E)
        qt_ref[0, h] = qn.T.astype(BF16)
        kn = halfnorm(k_ref[0, :, cols].astype(F32), kg_ref[...])
        ka_ref[0, h, :, 0:HEAD_DIM] = kn.astype(BF16)
        b_hi, b_mid, b_lo = (b.astype(F32) for b in _split3(pos * (slopes[h] * LOG2E)))
        bias = jnp.where(lane == 0, b_hi, jnp.where(lane == 1, b_mid, jnp.where(lane == 2, b_lo, 0.0)))
        ka_ref[0, h, :, HEAD_DIM:DIFF_KAUG] = bias.astype(BF16)
        for c in range(blk // DIFF_TK):
            vt_ref[0, h, c, 0:HEAD_DIM, :] = (
                v_ref[0, c * DIFF_TK:(c + 1) * DIFF_TK, cols].astype(F32).T.astype(BF16))
            vt_ref[0, h, c, HEAD_DIM:DIFF_VROWS, :] = ones_rows


def _diff_prep(p3, qg, kg, slopes, *, blk=512):
    bsz, t, _ = p3.shape
    kern = functools.partial(_diff_prep_kernel, blk=blk, slopes=slopes)
    return pl.pallas_call(
        kern,
        grid=(bsz, t // blk),
        in_specs=[pl.BlockSpec((1, blk, 512), lambda b, i: (b, i, _blk("cq", 512))),
                  pl.BlockSpec((1, blk, 512), lambda b, i: (b, i, _blk("ck", 512))),
                  pl.BlockSpec((1, blk, 512), lambda b, i: (b, i, _blk("cv", 512))),
                  pl.BlockSpec((1, HEAD_DIM), lambda b, i: (0, 0)),
                  pl.BlockSpec((1, HEAD_DIM), lambda b, i: (0, 0))],
        out_specs=[pl.BlockSpec((1, N_HEADS, HEAD_DIM, blk), lambda b, i: (b, 0, 0, i)),
                   pl.BlockSpec((1, N_HEADS, blk, DIFF_KAUG), lambda b, i: (b, 0, i, 0)),
                   pl.BlockSpec((1, N_HEADS, blk // DIFF_TK, DIFF_VROWS, DIFF_TK),
                                lambda b, i: (b, 0, i, 0, 0))],
        out_shape=[jax.ShapeDtypeStruct((bsz, N_HEADS, HEAD_DIM, t), BF16),
                   jax.ShapeDtypeStruct((bsz, N_HEADS, t, DIFF_KAUG), BF16),
                   jax.ShapeDtypeStruct((bsz, N_HEADS, t // DIFF_TK, DIFF_VROWS, DIFF_TK), BF16)],
        compiler_params=pltpu.CompilerParams(dimension_semantics=("parallel", "parallel"),
                                             vmem_limit_bytes=VMEM_LIMIT),
        name="diff_prep",
    )(p3, p3, p3, qg, kg)


def _diff_flash_kernel(qt_ref, ka_ref, vt_ref, z_ref, lam_ref, sg_ref, o_ref,
                       qa_ref, acc_ref, m_ref, sa_ref, sb_ref, ma_ref, mb_ref, *, tq, tk, lambda_init):
    qi = pl.program_id(2)
    hb = qt_ref.shape[1]
    r = lax.broadcasted_iota(jnp.int32, (HEAD_DIM, tq), 0)
    ones3 = jnp.where(r < 3, 1.0, 0.0).astype(BF16)
    for g in range(hb):
        qt = qt_ref[0, g]
        zero = jnp.zeros_like(qt)
        qa_ref[g, 0:HEAD_DIM, 0:tq] = jnp.where(r < DIFF_DQK, qt, zero)
        qa_ref[g, 0:HEAD_DIM, tq:2 * tq] = jnp.where(r < DIFF_DQK, zero, qt)
        qa_ref[g, HEAD_DIM:DIFF_KAUG, 0:tq] = ones3
        qa_ref[g, HEAD_DIM:DIFF_KAUG, tq:2 * tq] = ones3
    m_ref[...] = jnp.full_like(m_ref, -1e30)
    acc_ref[...] = jnp.zeros_like(acc_ref)

    groups = [(g, slice(c * MXU_COLS, (c + 1) * MXU_COLS), c * MXU_COLS)
              for c in range(2 * tq // MXU_COLS) for g in range(hb)]

    def produce(j, s_ref, mt_ref, grp, masked=False):
        g, cols, c0 = grp
        k0 = pl.multiple_of(j * tk, tk)
        s = _dot(ka_ref[0, g, pl.ds(k0, tk), :], qa_ref[g, :, cols])
        if masked:
            key = k0 + lax.broadcasted_iota(jnp.int32, (tk, MXU_COLS), 0)
            col = c0 + lax.broadcasted_iota(jnp.int32, (tk, MXU_COLS), 1)
            s = jnp.where(key <= qi * tq + jnp.bitwise_and(col, tq - 1), s, -jnp.inf)
        s_ref[g, :, cols] = s
        mt_ref[g, :, cols] = jnp.max(s, axis=0, keepdims=True)

    def consume(j, s_ref, mt_ref, grp, skip=None):
        g, cols, _ = grp
        m_old = m_ref[g, :, cols]
        mt = mt_ref[g, :, cols]
        m_new = jnp.maximum(m_old, mt if skip is None else mt - skip)
        alpha = jnp.exp2(m_old - m_new)
        m_sub = m_new if skip is None else m_new + skip
        p = jnp.exp2((s_ref[g, :, cols] - m_sub).astype(BF16))
        m_ref[g, :, cols] = m_new
        acc_ref[g, :, cols] = alpha * acc_ref[g, :, cols] + _dot(vt_ref[0, g, j], p)

    n = qi
    last = jnp.maximum(n - 1, 0)
    for grp in groups:
        produce(qi, sa_ref, ma_ref, grp, masked=True)
    for grp in groups:
        produce(0, sb_ref, mb_ref, grp)
        consume(qi, sa_ref, ma_ref, grp)

    def body(jj, carry):
        a = 2 * jj
        b = jnp.minimum(a + 1, last)
        skip_b = jnp.where(a + 1 < n, 0.0, SKIP_TILE)
        for grp in groups:
            produce(b, sa_ref, ma_ref, grp)
            consume(a, sb_ref, mb_ref, grp)
        for grp in groups:
            produce(jnp.minimum(a + 2, last), sb_ref, mb_ref, grp)
            consume(b, sa_ref, ma_ref, grp, skip=skip_b)
        return carry

    lax.fori_loop(0, (n + 1) // 2, body, 0)

    lam = lam_ref[...]
    lam_full = (jnp.exp(jnp.sum(lam[0:1] * lam[1:2], axis=-1, keepdims=True))
                - jnp.exp(jnp.sum(lam[2:3] * lam[3:4], axis=-1, keepdims=True)) + lambda_init)
    for g in range(hb):
        o_all = acc_ref[g, 0:HEAD_DIM, :] / acc_ref[g, HEAD_DIM:HEAD_DIM + 1, :]
        o = o_all[:, 0:tq] - lam_full * o_all[:, tq:2 * tq]
        var = jnp.mean(o * o, axis=0, keepdims=True)
        on = o * lax.rsqrt(var + NORM_EPS) * sg_ref[...] * (1.0 - lambda_init)
        cols = slice(g * HEAD_DIM, (g + 1) * HEAD_DIM)
        zf = z_ref[0, :, cols].astype(F32)
        o_ref[0, :, cols] = (on.T * _silu(zf)).astype(BF16)


def _diff_flash(qt, ka, vt, p3, lam, sg_col, lambda_init, *, tq=512, hb=2):
    bsz, _, _, t = qt.shape
    tk = DIFF_TK
    assert tq == tk and N_HEADS % hb == 0
    kern = functools.partial(_diff_flash_kernel, tq=tq, tk=tk, lambda_init=lambda_init)
    return pl.pallas_call(
        kern,
        grid=(bsz, N_HEADS // hb, t // tq),
        in_specs=[pl.BlockSpec((1, hb, HEAD_DIM, tq), lambda b, h, i: (b, h, 0, i)),
                  pl.BlockSpec((1, hb, t, DIFF_KAUG), lambda b, h, i: (b, h, 0, 0)),
                  pl.BlockSpec((1, hb, t // tk, DIFF_VROWS, tk), lambda b, h, i: (b, h, 0, 0, 0)),
                  pl.BlockSpec((1, tq, hb * HEAD_DIM), lambda b, h, i: (b, i, _blk("cz", hb * HEAD_DIM) + h)),
                  pl.BlockSpec((4, DIFF_DQK), lambda b, h, i: (0, 0)),
                  pl.BlockSpec((HEAD_DIM, 1), lambda b, h, i: (0, 0))],
        out_specs=pl.BlockSpec((1, tq, hb * HEAD_DIM), lambda b, h, i: (b, i, h)),
        out_shape=jax.ShapeDtypeStruct((bsz, t, GROUP_WIDTH), BF16),
        scratch_shapes=[pltpu.VMEM((hb, DIFF_KAUG, 2 * tq), BF16),
                        pltpu.VMEM((hb, DIFF_VROWS, 2 * tq), F32),
                        pltpu.VMEM((hb, 1, 2 * tq), F32),
                        pltpu.VMEM((hb, tk, 2 * tq), F32),
                        pltpu.VMEM((hb, tk, 2 * tq), F32),
                        pltpu.VMEM((hb, 1, 2 * tq), F32),
                        pltpu.VMEM((hb, 1, 2 * tq), F32)],
        compiler_params=pltpu.CompilerParams(dimension_semantics=("parallel", "parallel", "arbitrary"),
                                             vmem_limit_bytes=VMEM_LIMIT),
        name="diff_flash",
    )(qt, ka, vt, p3, lam, sg_col)


GDN_TAIL = 16
assert GROUP % (2 * CHUNK) == 0


def _gdn_kernel(qkv_ref, z_ref, ps_ref, cw_ref, alog_ref, dtb_ref, ng_ref, o_ref,
                xe_ref, s_ref, qn_ref, kn_ref, vv_ref, gate_ref, beta_ref, *, blk):
    @pl.when(pl.program_id(1) == 0)
    def _():
        s_ref[...] = jnp.zeros_like(s_ref)
        xe_ref[0:GDN_TAIL, :] = jnp.zeros((GDN_TAIL, 3 * GROUP_WIDTH), F32)

    xe_ref[GDN_TAIL:GDN_TAIL + blk, :] = qkv_ref[0].astype(F32)
    cw = cw_ref[...]
    y = cw[0:1, :] * xe_ref[GDN_TAIL - 3:GDN_TAIL - 3 + blk, :]
    for j in range(1, GDN_CONV):
        y = y + cw[j:j + 1, :] * xe_ref[GDN_TAIL - 3 + j:GDN_TAIL - 3 + j + blk, :]
    xe_ref[0:GDN_TAIL, :] = xe_ref[blk:blk + GDN_TAIL, :]
    y = _silu(y)

    def l2n(x):
        return x * lax.rsqrt(jnp.sum(x * x, axis=-1, keepdims=True) + NORM_EPS)

    for h in range(N_HEADS):
        cols = slice(h * HEAD_DIM, (h + 1) * HEAD_DIM)
        qn_ref[:, cols] = l2n(y[:, h * HEAD_DIM:(h + 1) * HEAD_DIM]) * (HEAD_DIM ** -0.5)
        kn_ref[:, cols] = l2n(y[:, GROUP_WIDTH + h * HEAD_DIM:GROUP_WIDTH + (h + 1) * HEAD_DIM])
    vv_ref[...] = y[:, 2 * GROUP_WIDTH:3 * GROUP_WIDTH]

    ps = ps_ref[0]
    gate_ref[...] = -jnp.exp(alog_ref[...]) * _softplus(ps + dtb_ref[...])
    beta_ref[...] = 1.0 / (1.0 + jnp.exp(-ps))

    gr = GROUP
    cpg = gr // CHUNK
    same, bd_tril, bd_strict = _chunk_block_masks(gr)
    tri_bd = jnp.where(bd_tril, 1.0, 0.0).astype(BF16)
    ones_bd = jnp.where(same, 1.0, 0.0).astype(BF16)
    eye = jnp.where(bd_tril, 1.0, 0.0) - jnp.where(bd_strict, 1.0, 0.0)
    ng = ng_ref[...]
    zeros_half = jnp.zeros((CHUNK, HEAD_DIM), BF16)

    n_groups = blk // gr
    probs = []
    for gi in range(n_groups):
        rows = slice(gi * gr, (gi + 1) * gr)
        g3 = _split3(gate_ref[rows, :])
        dec = _dot(tri_bd, g3[0]) + _dot(tri_bd, g3[1]) + _dot(tri_bd, g3[2])
        dec_end = _dot(ones_bd, g3[0]) + _dot(ones_bd, g3[1]) + _dot(ones_bd, g3[2])
        dec_t = dec.T
        e_dec = jnp.exp(dec)
        e_end = jnp.exp(dec_end - dec)
        e_tot = jnp.exp(dec_end)
        beta = beta_ref[rows, :]
        for h in range(N_HEADS):
            cols = slice(h * HEAD_DIM, (h + 1) * HEAD_DIM)
            la = SMALL_A + h
            diff = dec[:, la:la + 1] - dec_t[la:la + 1, :]
            dmask = jnp.where(bd_tril, jnp.exp(jnp.where(bd_tril, diff, 0.0)), 0.0)
            ecol = e_dec[:, la:la + 1]
            bcol = beta[:, SMALL_BETA + h:SMALL_BETA + h + 1]
            kn = kn_ref[rows, cols]
            qn = qn_ref[rows, cols]
            kb = kn * bcol
            kn_b = kn.astype(BF16)
            m = jnp.where(bd_strict, _dot_nt(kb.astype(BF16), kn_b) * dmask, 0.0)
            probs.append(dict(
                rows=rows, cols=cols, h=h, m=m,
                qk_b=(_dot_nt(qn.astype(BF16), kn_b) * dmask).astype(BF16),
                rhs=jnp.concatenate([(vv_ref[rows, cols] * bcol).astype(BF16), (kb * ecol).astype(BF16)], axis=1),
                qd_b=(qn * ecol).astype(BF16),
                ket_b=(kn * e_end[:, la:la + 1]).T.astype(BF16),
                d=[e_tot[ci * CHUNK:ci * CHUNK + 1, la:la + 1] for ci in range(cpg)]))

    tms = [eye - p["m"] for p in probs]
    pws = [p["m"] for p in probs]
    for _ in range(5):
        pws = [_dot(pw.astype(BF16), pw.astype(BF16)) for pw in pws]
        tms = [tm + _dot(tm.astype(BF16), pw.astype(BF16)) for tm, pw in zip(tms, pws)]
    for p, tm in zip(probs, tms):
        uw = _dot(tm.astype(BF16), p["rhs"])
        p["u"] = uw[:, 0:HEAD_DIM]
        p["w_b"] = uw[:, HEAD_DIM:2 * HEAD_DIM].astype(BF16)

    states = [s_ref[h] for h in range(N_HEADS)]
    for gi in range(n_groups):
        gp = probs[gi * N_HEADS:(gi + 1) * N_HEADS]
        v_news = [[] for _ in gp]
        o_inters = [[] for _ in gp]
        for ci in range(cpg):
            cr = slice(ci * CHUNK, (ci + 1) * CHUNK)
            pair = (ci // 2) * 2 * CHUNK
            for h, p in enumerate(gp):
                s_b = states[h].astype(BF16)
                v_new_b = (p["u"][cr] - _dot(p["w_b"][cr], s_b)).astype(BF16)
                o_inters[h].append(_dot(p["qd_b"][cr], s_b))
                v_news[h].append(v_new_b)
                vpad = jnp.concatenate([v_new_b, zeros_half] if ci % 2 == 0 else [zeros_half, v_new_b], axis=0)
                states[h] = states[h] * p["d"][ci] + _dot(p["ket_b"][:, pair:pair + 2 * CHUNK], vpad)
        for h, p in enumerate(gp):
            o = jnp.concatenate(o_inters[h], axis=0) + _dot(p["qk_b"], jnp.concatenate(v_news[h], axis=0))
            zf = z_ref[0, p["rows"], p["cols"]].astype(F32)
            o_ref[0, p["rows"], p["cols"]] = (_rms(o, ng) * _silu(zf)).astype(BF16)
    for h in range(N_HEADS):
        s_ref[h] = states[h]


def _gdn(p3, ps3, cw, alog_row, dtb_row, ng, *, blk=512):
    bsz, t, _ = p3.shape
    kern = functools.partial(_gdn_kernel, blk=blk)
    return pl.pallas_call(
        kern,
        grid=(bsz, t // blk),
        in_specs=[pl.BlockSpec((1, blk, 1536), lambda b, i: (b, i, _blk("dqkv", 1536))),
                  pl.BlockSpec((1, blk, 512), lambda b, i: (b, i, _blk("dz", 512))),
                  pl.BlockSpec((1, blk, LANES), lambda b, i: (b, i, 0)),
                  pl.BlockSpec((GDN_CONV, 1536), lambda b, i: (0, 0)),
                  pl.BlockSpec((1, LANES), lambda b, i: (0, 0)),
                  pl.BlockSpec((1, LANES), lambda b, i: (0, 0)),
                  pl.BlockSpec((1, HEAD_DIM), lambda b, i: (0, 0))],
        out_specs=pl.BlockSpec((1, blk, GROUP_WIDTH), lambda b, i: (b, i, 0)),
        out_shape=jax.ShapeDtypeStruct((bsz, t, GROUP_WIDTH), BF16),
        scratch_shapes=[pltpu.VMEM((GDN_TAIL + blk, 3 * GROUP_WIDTH), F32),
                        pltpu.VMEM((N_HEADS, HEAD_DIM, HEAD_DIM), F32),
                        pltpu.VMEM((blk, GROUP_WIDTH), F32),
                        pltpu.VMEM((blk, GROUP_WIDTH), F32),
                        pltpu.VMEM((blk, GROUP_WIDTH), F32),
                        pltpu.VMEM((blk, LANES), F32),
                        pltpu.VMEM((blk, LANES), F32)],
        compiler_params=pltpu.CompilerParams(dimension_semantics=("parallel", "arbitrary"),
                                             vmem_limit_bytes=VMEM_LIMIT),
        name="gdn",
    )(p3, p3, ps3, cw, alog_row, dtb_row, ng)


def _alibi_slopes():
    n = 2 * N_HEADS
    s = [2.0 ** (-8.0 * (i + 1) / n) for i in range(n)]
    return s[0::2], s[1::2]


def _pack_w_in(w_in):
    w_b = w_in.astype(BF16)
    cols = [w_b[..., _REF_OFF[n][0]:_REF_OFF[n][0] + _REF_OFF[n][1]] for n in _MAIN_ORDER]
    w_main = jnp.concatenate(cols, axis=-1)
    small = [w_b[..., _REF_OFF[n][0]:_REF_OFF[n][0] + _REF_OFF[n][1]] for n in ("alr", "dbeta", "da")]
    n_small = sum(s.shape[-1] for s in small)
    w_small = jnp.concatenate(small + [jnp.zeros(w_b.shape[:-1] + (LANES - n_small,), BF16)], axis=-1)
    return w_main, w_small


def _lane_row(vals, start):
    depth, n = vals.shape
    return jnp.pad(vals.astype(F32), ((0, 0), (start, LANES - start - n)))[:, None, :]


def kernel(x, norm_g, w_in, w_out, gla_w_lr2, gla_b_lr2, gla_norm_g, swa_q_norm_g, swa_k_norm_g, swa_sinks,
           diff_q_norm_g, diff_k_norm_g, diff_lambda, diff_subln_g, gdn_conv_w, gdn_A_log, gdn_dt_bias,
           gdn_norm_g):
    bsz, t, d = x.shape
    depth = w_in.shape[0]
    slopes_b, slopes_c = _alibi_slopes()
    w_main, w_small = _pack_w_in(w_in)
    w_out_b = w_out.astype(BF16)
    wlr = jnp.pad(gla_w_lr2, ((0, 0), (0, LANES - GLA_RANK), (0, 0))).astype(BF16)
    alog_row = _lane_row(gdn_A_log, SMALL_A)
    dtb_row = _lane_row(gdn_dt_bias, SMALL_A)
    slopes_b_arr = jnp.asarray(slopes_b, F32)

    x2 = x.reshape(bsz * t, d)
    for l in range(depth):
        lambda_init = 0.8 - 0.6 * math.exp(-0.3 * l)
        p_main, p_small = _inproj(x2, norm_g[l][None, :], w_main[l], w_small[l])
        p3 = p_main.reshape(bsz, t, MAIN_WIDTH)
        ps3 = p_small.reshape(bsz, t, LANES)
        ya = _gla(p3, ps3, wlr[l], gla_b_lr2[l][None, :], gla_norm_g[l][None, :])
        yb = _swa(p3, slopes_b_arr, swa_sinks[l], swa_q_norm_g[l][None, :], swa_k_norm_g[l][None, :])
        qt, ka, vt = _diff_prep(p3, jnp.tile(diff_q_norm_g[l], 2)[None, :],
                                jnp.tile(diff_k_norm_g[l], 2)[None, :], tuple(slopes_c))
        yc = _diff_flash(qt, ka, vt, p3, diff_lambda[l], diff_subln_g[l][:, None], lambda_init)
        yd = _gdn(p3, ps3, gdn_conv_w[l], alog_row[l], dtb_row[l], gdn_norm_g[l][None, :])
        ys = [y.reshape(bsz * t, GROUP_WIDTH) for y in (ya, yb, yc, yd)]
        x2 = _outproj(ys, x2, w_out_b[l])
    return x2.reshape(bsz, t, d)
```

```python
import functools
import math

import numpy as np
import jax
import jax.numpy as jnp
from jax import lax
from jax.experimental import pallas as pl
from jax.experimental.pallas import tpu as pltpu

F32 = jnp.float32
BF16 = jnp.bfloat16

D_MODEL = 2048
N_HEADS = 4
HEAD_DIM = 128
GROUP_WIDTH = 512
GLA_DK = 64
GLA_RANK = 16
GLA_GATE_NORM = 16.0
CHUNK = 64
GROUP = 256
SWA_WINDOW = 128
DIFF_DQK = 64
GDN_CONV = 4
NORM_EPS = 1e-6
LANES = 128
MXU_COLS = 256
VMEM_LIMIT = 56 * 1024 * 1024
INPROJ_CHUNK = 256

_REF_SIZES = (("aq", 256), ("ak", 256), ("av", 512), ("alr", 16), ("az", 512),
              ("bq", 512), ("bk", 256), ("bv", 256), ("bz", 512),
              ("cq", 512), ("ck", 512), ("cv", 512), ("cz", 512),
              ("dqkv", 1536), ("dbeta", 4), ("da", 4), ("dz", 512))
_REF_OFF = {}
_o = 0
for _n, _w in _REF_SIZES:
    _REF_OFF[_n] = (_o, _w)
    _o += _w
PROJ_WIDTH = _o

_MAIN_ORDER = ("aq", "ak", "av", "az", "bq", "bk", "bv", "bz", "cq", "ck", "cv", "cz", "dqkv", "dz")
_MAIN_OFF = {}
_o = 0
for _n in _MAIN_ORDER:
    _MAIN_OFF[_n] = _o
    _o += _REF_OFF[_n][1]
MAIN_WIDTH = _o
SMALL_LR, SMALL_BETA, SMALL_A = 0, 16, 20


def _blk(name, width):
    off = _MAIN_OFF[name]
    assert off % width == 0
    return off // width


def _silu(x):
    return x / (1.0 + jnp.exp(-x))


def _softplus(x):
    return jnp.maximum(x, 0.0) + jnp.log(1.0 + jnp.exp(-jnp.abs(x)))


def _rms(x, gain):
    return x * lax.rsqrt(jnp.mean(x * x, axis=-1, keepdims=True) + NORM_EPS) * gain


def _dot(a, b):
    return jnp.dot(a, b, preferred_element_type=F32)


def _dot_nt(a, b):
    return lax.dot_general(a, b, (((1,), (1,)), ((), ())), preferred_element_type=F32)


def _dot_tn(a, b):
    return lax.dot_general(a, b, (((0,), (0,)), ((), ())), preferred_element_type=F32)


def _split3(x):
    hi = x.astype(BF16)
    r1 = x - hi.astype(F32)
    mid = r1.astype(BF16)
    lo = (r1 - mid.astype(F32)).astype(BF16)
    return hi, mid, lo


def _cumsum_rows(tri, x):
    hi, mid, lo = _split3(x)
    return _dot(tri, hi) + _dot(tri, mid) + _dot(tri, lo)


def _tri_masks(n):
    r = lax.broadcasted_iota(jnp.int32, (n, n), 0)
    c = lax.broadcasted_iota(jnp.int32, (n, n), 1)
    return r >= c, r > c


def _chunk_block_masks(gr):
    r = lax.broadcasted_iota(jnp.int32, (gr, gr), 0)
    c = lax.broadcasted_iota(jnp.int32, (gr, gr), 1)
    u32 = functools.partial(lax.bitcast_convert_type, new_dtype=jnp.uint32)
    r_in = jnp.bitwise_and(r, CHUNK - 1)
    same = jnp.right_shift(r, CHUNK.bit_length() - 1) == jnp.right_shift(c, CHUNK.bit_length() - 1)
    bd_tril = u32(r - c) <= u32(r_in)
    bd_strict = u32(r - c - 1) < u32(r_in)
    return same, bd_tril, bd_strict


def _inproj_kernel(x_ref, g_ref, w_ref, ws_ref, p_ref, ps_ref, h_ref):
    @pl.when(pl.program_id(1) == 0)
    def _():
        tm = x_ref.shape[0]
        for c in range(tm // INPROJ_CHUNK):
            rows = slice(c * INPROJ_CHUNK, (c + 1) * INPROJ_CHUNK)
            h = _rms(x_ref[rows, :], g_ref[...]).astype(BF16)
            h_ref[rows, :] = h
            ps_ref[rows, :] = _dot(h, ws_ref[...])
            p_ref[rows, :] = _dot(h, w_ref[...]).astype(BF16)

    @pl.when(pl.program_id(1) != 0)
    def _():
        p_ref[...] = _dot(h_ref[...], w_ref[...]).astype(BF16)


def _inproj(x2, gain, w_main, w_small, *, tm=1024, tn=1024):
    m, d = x2.shape
    n = w_main.shape[1]
    return pl.pallas_call(
        _inproj_kernel,
        grid=(m // tm, n // tn),
        in_specs=[pl.BlockSpec((tm, d), lambda i, j: (i, 0)),
                  pl.BlockSpec((1, d), lambda i, j: (0, 0)),
                  pl.BlockSpec((d, tn), lambda i, j: (0, j)),
                  pl.BlockSpec((d, LANES), lambda i, j: (0, 0))],
        out_specs=[pl.BlockSpec((tm, tn), lambda i, j: (i, j)),
                   pl.BlockSpec((tm, LANES), lambda i, j: (i, 0))],
        out_shape=[jax.ShapeDtypeStruct((m, n), BF16), jax.ShapeDtypeStruct((m, LANES), F32)],
        scratch_shapes=[pltpu.VMEM((tm, d), BF16)],
        compiler_params=pltpu.CompilerParams(dimension_semantics=("parallel", "arbitrary"),
                                             vmem_limit_bytes=VMEM_LIMIT),
        name="inproj",
    )(x2, gain, w_main, w_small)


def _outproj_kernel(ya_ref, yb_ref, yc_ref, yd_ref, x_ref, w_ref, o_ref):
    acc = x_ref[...]
    for g, y_ref in enumerate((ya_ref, yb_ref, yc_ref, yd_ref)):
        acc = acc + _dot(y_ref[...], w_ref[g * GROUP_WIDTH:(g + 1) * GROUP_WIDTH, :])
    o_ref[...] = acc


def _outproj(ys, x2, w_out, *, tm=512, tn=D_MODEL):
    m, d = x2.shape
    yspec = pl.BlockSpec((tm, GROUP_WIDTH), lambda i, j: (i, 0))
    return pl.pallas_call(
        _outproj_kernel,
        grid=(m // tm, d // tn),
        in_specs=[yspec, yspec, yspec, yspec,
                  pl.BlockSpec((tm, tn), lambda i, j: (i, j)),
                  pl.BlockSpec((w_out.shape[0], tn), lambda i, j: (0, j))],
        out_specs=pl.BlockSpec((tm, tn), lambda i, j: (i, j)),
        out_shape=jax.ShapeDtypeStruct((m, d), F32),
        compiler_params=pltpu.CompilerParams(dimension_semantics=("parallel", "arbitrary"),
                                             vmem_limit_bytes=VMEM_LIMIT),
        name="outproj",
    )(*ys, x2, w_out)


def _gla_kernel(q_ref, k_ref, v_ref, z_ref, ps_ref, wlr_ref, blr_ref, ng_ref, o_ref, st_ref, *, blk):
    @pl.when(pl.program_id(1) == 0)
    def _():
        st_ref[...] = jnp.zeros_like(st_ref)

    logits = _dot(ps_ref[0].astype(BF16), wlr_ref[...]) + blr_ref[...]
    gates = -_softplus(-logits) * (1.0 / GLA_GATE_NORM)

    gr = GROUP
    cpg = gr // CHUNK
    same, bd_tril, _ = _chunk_block_masks(gr)
    tri_bd = jnp.where(bd_tril, 1.0, 0.0).astype(BF16)
    ones_bd = jnp.where(same, 1.0, 0.0).astype(BF16)
    ng = ng_ref[...]
    zeros_half = jnp.zeros((CHUNK, GLA_DK), BF16)
    states = [st_ref[h] for h in range(N_HEADS)]

    probs = []
    for gi in range(blk // gr):
        rows = slice(gi * gr, (gi + 1) * gr)
        g3 = _split3(gates[rows])
        b = _dot(tri_bd, g3[0]) + _dot(tri_bd, g3[1]) + _dot(tri_bd, g3[2])
        b_end = _dot(ones_bd, g3[0]) + _dot(ones_bd, g3[1]) + _dot(ones_bd, g3[2])
        qf = q_ref[0, rows, :].astype(F32)
        kf = k_ref[0, rows, :].astype(F32)
        q_dec = qf * (GLA_DK ** -0.5) * jnp.exp(b)
        k_inv = kf * jnp.exp(-b)
        k_end = kf * jnp.exp(b_end - b)
        decay = jnp.exp(b_end)
        for h in range(N_HEADS):
            ks = slice(h * GLA_DK, (h + 1) * GLA_DK)
            vs = slice(h * HEAD_DIM, (h + 1) * HEAD_DIM)
            probs.append(dict(h=h, rows=rows, vs=vs, qd=q_dec[:, ks].astype(BF16), ki=k_inv[:, ks].astype(BF16),
                              ke=k_end[:, ks].astype(BF16), v=v_ref[0, rows, vs],
                              d=[decay[ci * CHUNK:ci * CHUNK + 1, ks] for ci in range(cpg)]))
    for p in probs:
        p["a"] = jnp.where(bd_tril, _dot_nt(p["qd"], p["ki"]), 0.0).astype(BF16)
        vt = p["v"].astype(F32).T.astype(BF16)
        p["ut"] = []
        for ci in range(cpg):
            cr = slice(ci * CHUNK, (ci + 1) * CHUNK)
            pair = (ci // 2) * 2 * CHUNK
            ke_c = p["ke"][cr]
            ke_pad = jnp.concatenate([ke_c, zeros_half] if ci % 2 == 0 else [zeros_half, ke_c], axis=0)
            p["ut"].append(_dot(vt[:, pair:pair + 2 * CHUNK], ke_pad))
    for p in probs:
        h = p["h"]
        o_inters = []
        for ci in range(cpg):
            cr = slice(ci * CHUNK, (ci + 1) * CHUNK)
            o_inters.append(_dot_nt(p["qd"][cr], states[h].astype(BF16)))
            states[h] = states[h] * p["d"][ci] + p["ut"][ci]
        p["o"] = jnp.concatenate(o_inters, axis=0)
    for p in probs:
        o = _dot(p["a"], p["v"]) + p["o"]
        zf = z_ref[0, p["rows"], p["vs"]].astype(F32)
        o_ref[0, p["rows"], p["vs"]] = (_rms(o, ng) * _silu(zf)).astype(BF16)
    for h in range(N_HEADS):
        st_ref[h] = states[h]


def _gla(p3, ps3, wlr, blr, ng, *, blk=512):
    bsz, t, _ = p3.shape
    kern = functools.partial(_gla_kernel, blk=blk)
    return pl.pallas_call(
        kern,
        grid=(bsz, t // blk),
        in_specs=[pl.BlockSpec((1, blk, 256), lambda b, i: (b, i, _blk("aq", 256))),
                  pl.BlockSpec((1, blk, 256), lambda b, i: (b, i, _blk("ak", 256))),
                  pl.BlockSpec((1, blk, 512), lambda b, i: (b, i, _blk("av", 512))),
                  pl.BlockSpec((1, blk, 512), lambda b, i: (b, i, _blk("az", 512))),
                  pl.BlockSpec((1, blk, LANES), lambda b, i: (b, i, 0)),
                  pl.BlockSpec((LANES, 256), lambda b, i: (0, 0)),
                  pl.BlockSpec((1, 256), lambda b, i: (0, 0)),
                  pl.BlockSpec((1, HEAD_DIM), lambda b, i: (0, 0))],
        out_specs=pl.BlockSpec((1, blk, GROUP_WIDTH), lambda b, i: (b, i, 0)),
        out_shape=jax.ShapeDtypeStruct((bsz, t, GROUP_WIDTH), BF16),
        scratch_shapes=[pltpu.VMEM((N_HEADS, HEAD_DIM, GLA_DK), F32)],
        compiler_params=pltpu.CompilerParams(dimension_semantics=("parallel", "arbitrary"),
                                             vmem_limit_bytes=VMEM_LIMIT),
        name="gla",
    )(p3, p3, p3, p3, ps3, wlr, blr, ng)


def _swa_kernel(slopes_ref, sinks_ref, q_ref, k_ref, kp_ref, v_ref, vp_ref, z_ref, qg_ref, kg_ref,
                o_ref, *, tq):
    w = SWA_WINDOW
    g = pl.program_id(1)
    i = pl.program_id(2)
    kc = _rms(k_ref[0].astype(F32), kg_ref[...]).astype(BF16)
    kp = _rms(kp_ref[0].astype(F32), kg_ref[...]).astype(BF16)
    row = lax.broadcasted_iota(jnp.int32, (2 * w, 1), 0)
    first = row < w
    slope = jnp.where(first, slopes_ref[2 * g], slopes_ref[2 * g + 1])
    sink = jnp.where(first, sinks_ref[2 * g], sinks_ref[2 * g + 1])
    qi = jnp.bitwise_and(lax.broadcasted_iota(jnp.int32, (2 * w, 2 * w), 0), w - 1)
    kj = lax.broadcasted_iota(jnp.int32, (2 * w, 2 * w), 1)
    dist = qi + w - kj
    in_win = lax.bitcast_convert_type(dist, jnp.uint32) < w
    bias = slope * dist.astype(F32)
    for n in range(tq // w):
        rows = slice(n * w, (n + 1) * w)
        k_prev = kp if n == 0 else kc[(n - 1) * w:n * w]
        v_prev = vp_ref[0] if n == 0 else v_ref[0, (n - 1) * w:n * w, :]
        kb = jnp.concatenate([k_prev, kc[rows]], axis=0)
        vb = jnp.concatenate([v_prev, v_ref[0, rows, :]], axis=0)
        qf = q_ref[0, rows, :].astype(F32)
        qs = jnp.concatenate([qf[:, :HEAD_DIM], qf[:, HEAD_DIM:]], axis=0)
        qs = (_rms(qs, qg_ref[...]) * (HEAD_DIM ** -0.5)).astype(BF16)
        key_pos = (i * (tq // w) + n - 1) * w + kj
        s = jnp.where(in_win, _dot_nt(qs, kb) - bias, -jnp.inf)
        s = jnp.where(key_pos >= 0, s, -jnp.inf)
        mx = jnp.maximum(jnp.max(s, axis=-1, keepdims=True), sink)
        p = jnp.exp(s - mx)
        den = jnp.sum(p, axis=-1, keepdims=True) + jnp.exp(sink - mx)
        o = _dot(p.astype(BF16), vb) / den
        zf = z_ref[0, rows, :].astype(F32)
        o2 = jnp.concatenate([o[:w], o[w:]], axis=1)
        o_ref[0, rows, :] = (o2 * _silu(zf)).astype(BF16)


def _swa(p3, slopes, sinks, qg, kg, *, tq=512):
    bsz, t, _ = p3.shape
    w = SWA_WINDOW
    nb = tq // w
    kern = functools.partial(_swa_kernel, tq=tq)
    smem = pl.BlockSpec(memory_space=pltpu.SMEM)
    prev = lambda base: (lambda b, g, i: (b, jnp.maximum(i * nb - 1, 0), base + g))
    return pl.pallas_call(
        kern,
        grid=(bsz, 2, t // tq),
        in_specs=[smem, smem,
                  pl.BlockSpec((1, tq, 256), lambda b, g, i: (b, i, _blk("bq", 256) + g)),
                  pl.BlockSpec((1, tq, 128), lambda b, g, i: (b, i, _blk("bk", 128) + g)),
                  pl.BlockSpec((1, w, 128), prev(_blk("bk", 128))),
                  pl.BlockSpec((1, tq, 128), lambda b, g, i: (b, i, _blk("bv", 128) + g)),
                  pl.BlockSpec((1, w, 128), prev(_blk("bv", 128))),
                  pl.BlockSpec((1, tq, 256), lambda b, g, i: (b, i, _blk("bz", 256) + g)),
                  pl.BlockSpec((1, HEAD_DIM), lambda b, g, i: (0, 0)),
                  pl.BlockSpec((1, HEAD_DIM), lambda b, g, i: (0, 0))],
        out_specs=pl.BlockSpec((1, tq, 256), lambda b, g, i: (b, i, g)),
        out_shape=jax.ShapeDtypeStruct((bsz, t, GROUP_WIDTH), BF16),
        compiler_params=pltpu.CompilerParams(dimension_semantics=("parallel", "parallel", "arbitrary"),
                                             vmem_limit_bytes=VMEM_LIMIT),
        name="swa",
    )(slopes, sinks, p3, p3, p3, p3, p3, p3, qg, kg)


DIFF_KAUG = 256
DIFF_TK = 512
DIFF_VROWS = 144
LOG2E = 1.4426950408889634


def _diff_prep_kernel(q_ref, k_ref, v_ref, qg_ref, kg_ref, qt_ref, ka_ref, vt_ref, *, blk, slopes):
    i = pl.program_id(1)
    lane = lax.broadcasted_iota(jnp.int32, (blk, HEAD_DIM), 1)
    low = lane < DIFF_DQK
    pos = (i * blk + lax.broadcasted_iota(jnp.int32, (blk, HEAD_DIM), 0)).astype(F32)
    vrow = lax.broadcasted_iota(jnp.int32, (DIFF_VROWS - HEAD_DIM, DIFF_TK), 0)
    ones_rows = jnp.where(vrow == 0, 1.0, 0.0).astype(BF16)

    def halfnorm(x, gain):
        x2 = x * x
        s_lo = jnp.sum(jnp.where(low, x2, 0.0), axis=-1, keepdims=True)
        s_hi = jnp.sum(jnp.where(low, 0.0, x2), axis=-1, keepdims=True)
        ms = jnp.where(low, s_lo, s_hi) * (1.0 / DIFF_DQK)
        return x * lax.rsqrt(ms + NORM_EPS) * gain

    for h in range(N_HEADS):
        cols = slice(h * HEAD_DIM, (h + 1) * HEAD_DIM)
        qn = halfnorm(q_ref[0, :, cols].astype(F32), qg_ref[...]) * (DIFF_DQK ** -0.5 * LOG2E)
        qt_ref[0, h] = qn.T.astype(BF16)
        kn = halfnorm(k_ref[0, :, cols].astype(F32), kg_ref[...])
        ka_ref[0, h, :, 0:HEAD_DIM] = kn.astype(BF16)
        b_hi, b_mid, b_lo = (b.astype(F32) for b in _split3(pos * (slopes[h] * LOG2E)))
        bias = jnp.where(lane == 0, b_hi, jnp.where(lane == 1, b_mid, jnp.where(lane == 2, b_lo, 0.0)))
        ka_ref[0, h, :, HEAD_DIM:DIFF_KAUG] = bias.astype(BF16)
        for c in range(blk // DIFF_TK):
            vt_ref[0, h, c, 0:HEAD_DIM, :] = (
                v_ref[0, c * DIFF_TK:(c + 1) * DIFF_TK, cols].astype(F32).T.astype(BF16))
            vt_ref[0, h, c, HEAD_DIM:DIFF_VROWS, :] = ones_rows


def _diff_prep(p3, qg, kg, slopes, *, blk=512):
    bsz, t, _ = p3.shape
    kern = functools.partial(_diff_prep_kernel, blk=blk, slopes=slopes)
    return pl.pallas_call(
        kern,
        grid=(bsz, t // blk),
        in_specs=[pl.BlockSpec((1, blk, 512), lambda b, i: (b, i, _blk("cq", 512))),
                  pl.BlockSpec((1, blk, 512), lambda b, i: (b, i, _blk("ck", 512))),
                  pl.BlockSpec((1, blk, 512), lambda b, i: (b, i, _blk("cv", 512))),
                  pl.BlockSpec((1, HEAD_DIM), lambda b, i: (0, 0)),
                  pl.BlockSpec((1, HEAD_DIM), lambda b, i: (0, 0))],
        out_specs=[pl.BlockSpec((1, N_HEADS, HEAD_DIM, blk), lambda b, i: (b, 0, 0, i)),
                   pl.BlockSpec((1, N_HEADS, blk, DIFF_KAUG), lambda b, i: (b, 0, i, 0)),
                   pl.BlockSpec((1, N_HEADS, blk // DIFF_TK, DIFF_VROWS, DIFF_TK),
                                lambda b, i: (b, 0, i, 0, 0))],
        out_shape=[jax.ShapeDtypeStruct((bsz, N_HEADS, HEAD_DIM, t), BF16),
                   jax.ShapeDtypeStruct((bsz, N_HEADS, t, DIFF_KAUG), BF16),
                   jax.ShapeDtypeStruct((bsz, N_HEADS, t // DIFF_TK, DIFF_VROWS, DIFF_TK), BF16)],
        compiler_params=pltpu.CompilerParams(dimension_semantics=("parallel", "parallel"),
                                             vmem_limit_bytes=VMEM_LIMIT),
        name="diff_prep",
    )(p3, p3, p3, qg, kg)


def _diff_flash_kernel(qt_ref, ka_ref, vt_ref, z_ref, lam_ref, sg_ref, o_ref,
                       qa_ref, acc_ref, m_ref, sa_ref, sb_ref, ma_ref, mb_ref, *, tq, tk, lambda_init):
    qi = pl.program_id(2)
    hb = qt_ref.shape[1]
    r = lax.broadcasted_iota(jnp.int32, (HEAD_DIM, tq), 0)
    ones3 = jnp.where(r < 3, 1.0, 0.0).astype(BF16)
    for g in range(hb):
        qt = qt_ref[0, g]
        zero = jnp.zeros_like(qt)
        qa_ref[g, 0:HEAD_DIM, 0:tq] = jnp.where(r < DIFF_DQK, qt, zero)
        qa_ref[g, 0:HEAD_DIM, tq:2 * tq] = jnp.where(r < DIFF_DQK, zero, qt)
        qa_ref[g, HEAD_DIM:DIFF_KAUG, 0:tq] = ones3
        qa_ref[g, HEAD_DIM:DIFF_KAUG, tq:2 * tq] = ones3
    m_ref[...] = jnp.full_like(m_ref, -1e30)
    acc_ref[...] = jnp.zeros_like(acc_ref)

    groups = [(g, slice(c * MXU_COLS, (c + 1) * MXU_COLS), c * MXU_COLS)
              for c in range(2 * tq // MXU_COLS) for g in range(hb)]

    def produce(j, s_ref, mt_ref, grp, masked=False):
        g, cols, c0 = grp
        k0 = pl.multiple_of(j * tk, tk)
        s = _dot(ka_ref[0, g, pl.ds(k0, tk), :], qa_ref[g, :, cols])
        if masked:
            key = k0 + lax.broadcasted_iota(jnp.int32, (tk, MXU_COLS), 0)
            col = c0 + lax.broadcasted_iota(jnp.int32, (tk, MXU_COLS), 1)
            s = jnp.where(key <= qi * tq + jnp.bitwise_and(col, tq - 1), s, -jnp.inf)
        s_ref[g, :, cols] = s
        mt_ref[g, :, cols] = jnp.max(s, axis=0, keepdims=True)

    def consume(j, s_ref, mt_ref, grp):
        g, cols, _ = grp
        m_old = m_ref[g, :, cols]
        m_new = jnp.maximum(m_old, mt_ref[g, :, cols])
        alpha = jnp.exp2(m_old - m_new)
        p = jnp.exp2((s_ref[g, :, cols] - m_new).astype(BF16))
        m_ref[g, :, cols] = m_new
        acc_ref[g, :, cols] = alpha * acc_ref[g, :, cols] + _dot(vt_ref[0, g, j], p)

    n = qi
    last = jnp.maximum(n - 1, 0)
    for grp in groups:
        produce(qi, sa_ref, ma_ref, grp, masked=True)
    for grp in groups:
        produce(0, sb_ref, mb_ref, grp)
        consume(qi, sa_ref, ma_ref, grp)

    def body(jj, carry):
        a = 2 * jj
        for grp in groups:
            produce(a + 1, sa_ref, ma_ref, grp)
            consume(a, sb_ref, mb_ref, grp)
        for grp in groups:
            produce(jnp.minimum(a + 2, last), sb_ref, mb_ref, grp)
            consume(a + 1, sa_ref, ma_ref, grp)
        return carry

    lax.fori_loop(0, n // 2, body, 0)

    @pl.when(n % 2 == 1)
    def _():
        for grp in groups:
            consume(n - 1, sb_ref, mb_ref, grp)

    lam = lam_ref[...]
    lam_full = (jnp.exp(jnp.sum(lam[0:1] * lam[1:2], axis=-1, keepdims=True))
                - jnp.exp(jnp.sum(lam[2:3] * lam[3:4], axis=-1, keepdims=True)) + lambda_init)
    for g in range(hb):
        o_all = acc_ref[g, 0:HEAD_DIM, :] / acc_ref[g, HEAD_DIM:HEAD_DIM + 1, :]
        o = o_all[:, 0:tq] - lam_full * o_all[:, tq:2 * tq]
        var = jnp.mean(o * o, axis=0, keepdims=True)
        on = o * lax.rsqrt(var + NORM_EPS) * sg_ref[...] * (1.0 - lambda_init)
        cols = slice(g * HEAD_DIM, (g + 1) * HEAD_DIM)
        zf = z_ref[0, :, cols].astype(F32)
        o_ref[0, :, cols] = (on.T * _silu(zf)).astype(BF16)


def _diff_flash(qt, ka, vt, p3, lam, sg_col, lambda_init, *, tq=512, hb=2):
    bsz, _, _, t = qt.shape
    tk = DIFF_TK
    assert tq == tk and N_HEADS % hb == 0
    kern = functools.partial(_diff_flash_kernel, tq=tq, tk=tk, lambda_init=lambda_init)
    return pl.pallas_call(
        kern,
        grid=(bsz, N_HEADS // hb, t // tq),
        in_specs=[pl.BlockSpec((1, hb, HEAD_DIM, tq), lambda b, h, i: (b, h, 0, i)),
                  pl.BlockSpec((1, hb, t, DIFF_KAUG), lambda b, h, i: (b, h, 0, 0)),
                  pl.BlockSpec((1, hb, t // tk, DIFF_VROWS, tk), lambda b, h, i: (b, h, 0, 0, 0)),
                  pl.BlockSpec((1, tq, hb * HEAD_DIM), lambda b, h, i: (b, i, _blk("cz", hb * HEAD_DIM) + h)),
                  pl.BlockSpec((4, DIFF_DQK), lambda b, h, i: (0, 0)),
                  pl.BlockSpec((HEAD_DIM, 1), lambda b, h, i: (0, 0))],
        out_specs=pl.BlockSpec((1, tq, hb * HEAD_DIM), lambda b, h, i: (b, i, h)),
        out_shape=jax.ShapeDtypeStruct((bsz, t, GROUP_WIDTH), BF16),
        scratch_shapes=[pltpu.VMEM((hb, DIFF_KAUG, 2 * tq), BF16),
                        pltpu.VMEM((hb, DIFF_VROWS, 2 * tq), F32),
                        pltpu.VMEM((hb, 1, 2 * tq), F32),
                        pltpu.VMEM((hb, tk, 2 * tq), F32),
                        pltpu.VMEM((hb, tk, 2 * tq), F32),
                        pltpu.VMEM((hb, 1, 2 * tq), F32),
                        pltpu.VMEM((hb, 1, 2 * tq), F32)],
        compiler_params=pltpu.CompilerParams(dimension_semantics=("parallel", "parallel", "arbitrary"),
                                             vmem_limit_bytes=VMEM_LIMIT),
        name="diff_flash",
    )(qt, ka, vt, p3, lam, sg_col)


GDN_TAIL = 16
GDN_UNIT = 2
assert GROUP % (2 * CHUNK) == 0


def _gdn_kernel(q_ref, k_ref, v_ref, z_ref, ps_ref, cw_ref, alog_ref, dtb_ref, ng_ref, o_ref,
                xe_ref, s_ref, qn_ref, kn_ref, vv_ref, gate_ref, beta_ref, *, blk):
    @pl.when(pl.program_id(1) == 0)
    def _():
        s_ref[...] = jnp.zeros_like(s_ref)
        xe_ref[0:GDN_TAIL, :] = jnp.zeros((GDN_TAIL, 3 * GROUP_WIDTH), F32)

    gr = GROUP
    n_groups = blk // gr
    cw = cw_ref[...]

    def l2n(x):
        return x * lax.rsqrt(jnp.sum(x * x, axis=-1, keepdims=True) + NORM_EPS)

    def conv_pieces(gi):
        rows = slice(gi * gr, (gi + 1) * gr)
        base = GDN_TAIL + gi * gr

        def piece(cb):
            cs = slice(cb * LANES, (cb + 1) * LANES)
            src = (q_ref, k_ref, v_ref)[cb // N_HEADS]
            hs = slice((cb % N_HEADS) * LANES, (cb % N_HEADS + 1) * LANES)
            xe_ref[base:base + gr, cs] = src[0, rows, hs].astype(F32)
            y = cw[0:1, cs] * xe_ref[base - 3:base - 3 + gr, cs]
            for j in range(1, GDN_CONV):
                y = y + cw[j:j + 1, cs] * xe_ref[base - 3 + j:base - 3 + j + gr, cs]
            y = _silu(y)
            if cb < N_HEADS:
                qn_ref[rows, cs] = l2n(y) * (HEAD_DIM ** -0.5)
            elif cb < 2 * N_HEADS:
                kn_ref[rows, (cb - N_HEADS) * LANES:(cb - N_HEADS + 1) * LANES] = l2n(y)
            else:
                vv_ref[rows, (cb - 2 * N_HEADS) * LANES:(cb - 2 * N_HEADS + 1) * LANES] = y

        return [functools.partial(piece, cb) for cb in range(3 * N_HEADS)]

    ps = ps_ref[0]
    gate_ref[...] = -jnp.exp(alog_ref[...]) * _softplus(ps + dtb_ref[...])
    beta_ref[...] = 1.0 / (1.0 + jnp.exp(-ps))

    cpg = gr // CHUNK
    same, bd_tril, bd_strict = _chunk_block_masks(gr)
    tri_bd = jnp.where(bd_tril, 1.0, 0.0).astype(BF16)
    ones_bd = jnp.where(same, 1.0, 0.0).astype(BF16)
    eye = jnp.where(bd_tril, 1.0, 0.0) - jnp.where(bd_strict, 1.0, 0.0)
    ng = ng_ref[...]
    zeros_half = jnp.zeros((CHUNK, HEAD_DIM), BF16)

    def setup(gi):
        probs = []
        rows = slice(gi * gr, (gi + 1) * gr)
        g3 = _split3(gate_ref[rows, :])
        dec = _dot(tri_bd, g3[0]) + _dot(tri_bd, g3[1]) + _dot(tri_bd, g3[2])
        dec_end = _dot(ones_bd, g3[0]) + _dot(ones_bd, g3[1]) + _dot(ones_bd, g3[2])
        dec_t = dec.T
        e_dec = jnp.exp(dec)
        e_end = jnp.exp(dec_end - dec)
        e_tot = jnp.exp(dec_end)
        beta = beta_ref[rows, :]
        for h in range(N_HEADS):
            cols = slice(h * HEAD_DIM, (h + 1) * HEAD_DIM)
            la = SMALL_A + h
            diff = dec[:, la:la + 1] - dec_t[la:la + 1, :]
            dmask = jnp.where(bd_tril, jnp.exp(jnp.where(bd_tril, diff, 0.0)), 0.0)
            ecol = e_dec[:, la:la + 1]
            bcol = beta[:, SMALL_BETA + h:SMALL_BETA + h + 1]
            kn = kn_ref[rows, cols]
            qn = qn_ref[rows, cols]
            kb = kn * bcol
            kn_b = kn.astype(BF16)
            m = jnp.where(bd_strict, _dot_nt(kb.astype(BF16), kn_b) * dmask, 0.0)
            probs.append(dict(
                rows=rows, cols=cols, tm=eye - m, pw=m,
                qk_b=(_dot_nt(qn.astype(BF16), kn_b) * dmask).astype(BF16),
                rhs=jnp.concatenate([(vv_ref[rows, cols] * bcol).astype(BF16), (kb * ecol).astype(BF16)], axis=1),
                qd_b=(qn * ecol).astype(BF16),
                ket_b=(kn * e_end[:, la:la + 1]).T.astype(BF16),
                d=[e_tot[ci * CHUNK:ci * CHUNK + 1, la:la + 1] for ci in range(cpg)]))
        return probs

    def inverse_stages(probs):
        def stage():
            for p in probs:
                pw_b = p["pw"].astype(BF16)
                p["pw"] = _dot(pw_b, pw_b)
            for p in probs:
                p["tm"] = p["tm"] + _dot(p["tm"].astype(BF16), p["pw"].astype(BF16))

        def finish():
            for p in probs:
                uw = _dot(p["tm"].astype(BF16), p["rhs"])
                p["u"] = uw[:, 0:HEAD_DIM]
                p["w_b"] = uw[:, HEAD_DIM:2 * HEAD_DIM].astype(BF16)

        return [stage] * 5 + [finish]

    states = [s_ref[h] for h in range(N_HEADS)]

    def scan_steps(probs):
        v_news = [[] for _ in probs]
        o_inters = [[] for _ in probs]

        def step(ci):
            cr = slice(ci * CHUNK, (ci + 1) * CHUNK)
            pair = (ci // 2) * 2 * CHUNK
            for h, p in enumerate(probs):
                s_b = states[h].astype(BF16)
                v_new_b = (p["u"][cr] - _dot(p["w_b"][cr], s_b)).astype(BF16)
                o_inters[h].append(_dot(p["qd_b"][cr], s_b))
                v_news[h].append(v_new_b)
                vpad = jnp.concatenate([v_new_b, zeros_half] if ci % 2 == 0 else [zeros_half, v_new_b], axis=0)
                states[h] = states[h] * p["d"][ci] + _dot(p["ket_b"][:, pair:pair + 2 * CHUNK], vpad)

        def finish():
            for h, p in enumerate(probs):
                o = jnp.concatenate(o_inters[h], axis=0) + _dot(p["qk_b"], jnp.concatenate(v_news[h], axis=0))
                zf = z_ref[0, p["rows"], p["cols"]].astype(F32)
                o_ref[0, p["rows"], p["cols"]] = (_rms(o, ng) * _silu(zf)).astype(BF16)

        return [functools.partial(step, ci) for ci in range(cpg)] + [finish]

    units = [list(range(u, min(u + GDN_UNIT, n_groups))) for u in range(0, n_groups, GDN_UNIT)]

    def unit_conv(unit):
        return [piece for gi in unit for piece in conv_pieces(gi)]

    def unit_scan(unit_probs):
        return [thunk for probs in unit_probs for thunk in scan_steps(probs)]

    for piece in unit_conv(units[0]):
        piece()
    prev = None
    for ui, unit in enumerate(units):
        unit_probs = [setup(gi) for gi in unit]
        streams = [inverse_stages([p for probs in unit_probs for p in probs]),
                   unit_conv(units[ui + 1]) if ui + 1 < len(units) else [],
                   unit_scan(prev) if prev is not None else []]
        rates = [1, 4, 2]
        while any(streams):
            for k, rate in enumerate(rates):
                for _ in range(rate):
                    if streams[k]:
                        streams[k].pop(0)()
        prev = unit_probs
    for thunk in unit_scan(prev):
        thunk()
    xe_ref[0:GDN_TAIL, :] = xe_ref[blk:blk + GDN_TAIL, :]
    for h in range(N_HEADS):
        s_ref[h] = states[h]


def _gdn(p3, ps3, cw, alog_row, dtb_row, ng, *, blk=512):
    bsz, t, _ = p3.shape
    kern = functools.partial(_gdn_kernel, blk=blk)
    return pl.pallas_call(
        kern,
        grid=(bsz, t // blk),
        in_specs=[pl.BlockSpec((1, blk, 512), lambda b, i: (b, i, _blk("dqkv", 512))),
                  pl.BlockSpec((1, blk, 512), lambda b, i: (b, i, _blk("dqkv", 512) + 1)),
                  pl.BlockSpec((1, blk, 512), lambda b, i: (b, i, _blk("dqkv", 512) + 2)),
                  pl.BlockSpec((1, blk, 512), lambda b, i: (b, i, _blk("dz", 512))),
                  pl.BlockSpec((1, blk, LANES), lambda b, i: (b, i, 0)),
                  pl.BlockSpec((GDN_CONV, 1536), lambda b, i: (0, 0)),
                  pl.BlockSpec((1, LANES), lambda b, i: (0, 0)),
                  pl.BlockSpec((1, LANES), lambda b, i: (0, 0)),
                  pl.BlockSpec((1, HEAD_DIM), lambda b, i: (0, 0))],
        out_specs=pl.BlockSpec((1, blk, GROUP_WIDTH), lambda b, i: (b, i, 0)),
        out_shape=jax.ShapeDtypeStruct((bsz, t, GROUP_WIDTH), BF16),
        scratch_shapes=[pltpu.VMEM((GDN_TAIL + blk, 3 * GROUP_WIDTH), F32),
                        pltpu.VMEM((N_HEADS, HEAD_DIM, HEAD_DIM), F32),
                        pltpu.VMEM((blk, GROUP_WIDTH), F32),
                        pltpu.VMEM((blk, GROUP_WIDTH), F32),
                        pltpu.VMEM((blk, GROUP_WIDTH), F32),
                        pltpu.VMEM((blk, LANES), F32),
                        pltpu.VMEM((blk, LANES), F32)],
        compiler_params=pltpu.CompilerParams(dimension_semantics=("parallel", "arbitrary"),
                                             vmem_limit_bytes=VMEM_LIMIT),
        name="gdn",
    )(p3, p3, p3, p3, ps3, cw, alog_row, dtb_row, ng)


def _alibi_slopes():
    n = 2 * N_HEADS
    s = [2.0 ** (-8.0 * (i + 1) / n) for i in range(n)]
    return s[0::2], s[1::2]


def _pack_w_in(w_in):
    def ref_range(first, last):
        return w_in[..., _REF_OFF[first][0]:_REF_OFF[last][0] + _REF_OFF[last][1]]

    w_main = jnp.concatenate([ref_range("aq", "av"), ref_range("az", "dqkv"), ref_range("dz", "dz")],
                             axis=-1).astype(BF16)
    assert w_main.shape[-1] == MAIN_WIDTH
    small = [ref_range(n, n) for n in ("alr", "dbeta", "da")]
    n_small = sum(s.shape[-1] for s in small)
    w_small = jnp.concatenate(small + [jnp.zeros(w_in.shape[:-1] + (LANES - n_small,), w_in.dtype)],
                              axis=-1).astype(BF16)
    return w_main, w_small


def _lane_row(vals, start):
    depth, n = vals.shape
    return jnp.pad(vals.astype(F32), ((0, 0), (start, LANES - start - n)))[:, None, :]


def kernel(x, norm_g, w_in, w_out, gla_w_lr2, gla_b_lr2, gla_norm_g, swa_q_norm_g, swa_k_norm_g, swa_sinks,
           diff_q_norm_g, diff_k_norm_g, diff_lambda, diff_subln_g, gdn_conv_w, gdn_A_log, gdn_dt_bias,
           gdn_norm_g):
    bsz, t, d = x.shape
    depth = w_in.shape[0]
    slopes_b, slopes_c = _alibi_slopes()
    w_main, w_small = _pack_w_in(w_in)
    w_out_b = w_out.astype(BF16)
    wlr = jnp.pad(gla_w_lr2, ((0, 0), (0, LANES - GLA_RANK), (0, 0))).astype(BF16)
    alog_row = _lane_row(gdn_A_log, SMALL_A)
    dtb_row = _lane_row(gdn_dt_bias, SMALL_A)
    slopes_b_arr = jnp.asarray(slopes_b, F32)

    x2 = x.reshape(bsz * t, d)
    for l in range(depth):
        lambda_init = 0.8 - 0.6 * math.exp(-0.3 * l)
        p_main, p_small = _inproj(x2, norm_g[l][None, :], w_main[l], w_small[l])
        p3 = p_main.reshape(bsz, t, MAIN_WIDTH)
        ps3 = p_small.reshape(bsz, t, LANES)
        ya = _gla(p3, ps3, wlr[l], gla_b_lr2[l][None, :], gla_norm_g[l][None, :])
        yb = _swa(p3, slopes_b_arr, swa_sinks[l], swa_q_norm_g[l][None, :], swa_k_norm_g[l][None, :])
        qt, ka, vt = _diff_prep(p3, jnp.tile(diff_q_norm_g[l], 2)[None, :],
                                jnp.tile(diff_k_norm_g[l], 2)[None, :], tuple(slopes_c))
        yc = _diff_flash(qt, ka, vt, p3, diff_lambda[l], diff_subln_g[l][:, None], lambda_init)
        yd = _gdn(p3, ps3, gdn_conv_w[l], alog_row[l], dtb_row[l], gdn_norm_g[l][None, :])
        ys = [y.reshape(bsz * t, GROUP_WIDTH) for y in (ya, yb, yc, yd)]
        x2 = _outproj(ys, x2, w_out_b[l])
    return x2.reshape(bsz, t, d)
```

```python
import functools
import math

import numpy as np
import jax
import jax.numpy as jnp
from jax import lax
from jax.experimental import pallas as pl
from jax.experimental.pallas import tpu as pltpu

F32 = jnp.float32
BF16 = jnp.bfloat16

D_MODEL = 2048
N_HEADS = 4
HEAD_DIM = 128
GROUP_WIDTH = 512
GLA_DK = 64
GLA_RANK = 16
GLA_GATE_NORM = 16.0
CHUNK = 64
GROUP = 256
SWA_WINDOW = 128
DIFF_DQK = 64
GDN_CONV = 4
NORM_EPS = 1e-6
LANES = 128
MXU_COLS = 256
VMEM_LIMIT = 56 * 1024 * 1024
INPROJ_CHUNK = 256

_REF_SIZES = (("aq", 256), ("ak", 256), ("av", 512), ("alr", 16), ("az", 512),
              ("bq", 512), ("bk", 256), ("bv", 256), ("bz", 512),
              ("cq", 512), ("ck", 512), ("cv", 512), ("cz", 512),
              ("dqkv", 1536), ("dbeta", 4), ("da", 4), ("dz", 512))
_REF_OFF = {}
_o = 0
for _n, _w in _REF_SIZES:
    _REF_OFF[_n] = (_o, _w)
    _o += _w
PROJ_WIDTH = _o

_MAIN_ORDER = ("aq", "ak", "av", "az", "bq", "bk", "bv", "bz", "cq", "ck", "cv", "cz", "dqkv", "dz")
_MAIN_OFF = {}
_o = 0
for _n in _MAIN_ORDER:
    _MAIN_OFF[_n] = _o
    _o += _REF_OFF[_n][1]
MAIN_WIDTH = _o
SMALL_LR, SMALL_BETA, SMALL_A = 0, 16, 20


def _blk(name, width):
    off = _MAIN_OFF[name]
    assert off % width == 0
    return off // width


def _silu(x):
    return x / (1.0 + jnp.exp(-x))


def _softplus(x):
    return jnp.maximum(x, 0.0) + jnp.log(1.0 + jnp.exp(-jnp.abs(x)))


def _rms(x, gain):
    return x * lax.rsqrt(jnp.mean(x * x, axis=-1, keepdims=True) + NORM_EPS) * gain


def _dot(a, b):
    return jnp.dot(a, b, preferred_element_type=F32)


def _dot_nt(a, b):
    return lax.dot_general(a, b, (((1,), (1,)), ((), ())), preferred_element_type=F32)


def _dot_tn(a, b):
    return lax.dot_general(a, b, (((0,), (0,)), ((), ())), preferred_element_type=F32)


def _split3(x):
    hi = x.astype(BF16)
    r1 = x - hi.astype(F32)
    mid = r1.astype(BF16)
    lo = (r1 - mid.astype(F32)).astype(BF16)
    return hi, mid, lo


def _cumsum_rows(tri, x):
    hi, mid, lo = _split3(x)
    return _dot(tri, hi) + _dot(tri, mid) + _dot(tri, lo)


def _tri_masks(n):
    r = lax.broadcasted_iota(jnp.int32, (n, n), 0)
    c = lax.broadcasted_iota(jnp.int32, (n, n), 1)
    return r >= c, r > c


def _chunk_block_masks(gr):
    r = lax.broadcasted_iota(jnp.int32, (gr, gr), 0)
    c = lax.broadcasted_iota(jnp.int32, (gr, gr), 1)
    u32 = functools.partial(lax.bitcast_convert_type, new_dtype=jnp.uint32)
    r_in = jnp.bitwise_and(r, CHUNK - 1)
    same = jnp.right_shift(r, CHUNK.bit_length() - 1) == jnp.right_shift(c, CHUNK.bit_length() - 1)
    bd_tril = u32(r - c) <= u32(r_in)
    bd_strict = u32(r - c - 1) < u32(r_in)
    return same, bd_tril, bd_strict


def _inproj_kernel(x_ref, g_ref, w_ref, ws_ref, p_ref, ps_ref, h_ref):
    @pl.when(pl.program_id(1) == 0)
    def _():
        tm = x_ref.shape[0]
        for c in range(tm // INPROJ_CHUNK):
            rows = slice(c * INPROJ_CHUNK, (c + 1) * INPROJ_CHUNK)
            h = _rms(x_ref[rows, :], g_ref[...]).astype(BF16)
            h_ref[rows, :] = h
            ps_ref[rows, :] = _dot(h, ws_ref[...])
            p_ref[rows, :] = _dot(h, w_ref[...]).astype(BF16)

    @pl.when(pl.program_id(1) != 0)
    def _():
        p_ref[...] = _dot(h_ref[...], w_ref[...]).astype(BF16)


def _inproj(x2, gain, w_main, w_small, *, tm=1024, tn=1024):
    m, d = x2.shape
    n = w_main.shape[1]
    return pl.pallas_call(
        _inproj_kernel,
        grid=(m // tm, n // tn),
        in_specs=[pl.BlockSpec((tm, d), lambda i, j: (i, 0)),
                  pl.BlockSpec((1, d), lambda i, j: (0, 0)),
                  pl.BlockSpec((d, tn), lambda i, j: (0, j)),
                  pl.BlockSpec((d, LANES), lambda i, j: (0, 0))],
        out_specs=[pl.BlockSpec((tm, tn), lambda i, j: (i, j)),
                   pl.BlockSpec((tm, LANES), lambda i, j: (i, 0))],
        out_shape=[jax.ShapeDtypeStruct((m, n), BF16), jax.ShapeDtypeStruct((m, LANES), F32)],
        scratch_shapes=[pltpu.VMEM((tm, d), BF16)],
        compiler_params=pltpu.CompilerParams(dimension_semantics=("parallel", "arbitrary"),
                                             vmem_limit_bytes=VMEM_LIMIT),
        name="inproj",
    )(x2, gain, w_main, w_small)


def _outproj_kernel(ya_ref, yb_ref, yc_ref, yd_ref, x_ref, w_ref, o_ref):
    acc = x_ref[...]
    for g, y_ref in enumerate((ya_ref, yb_ref, yc_ref, yd_ref)):
        acc = acc + _dot(y_ref[...], w_ref[g * GROUP_WIDTH:(g + 1) * GROUP_WIDTH, :])
    o_ref[...] = acc


def _outproj(ys, x2, w_out, *, tm=512, tn=D_MODEL):
    m, d = x2.shape
    yspec = pl.BlockSpec((tm, GROUP_WIDTH), lambda i, j: (i, 0))
    return pl.pallas_call(
        _outproj_kernel,
        grid=(m // tm, d // tn),
        in_specs=[yspec, yspec, yspec, yspec,
                  pl.BlockSpec((tm, tn), lambda i, j: (i, j)),
                  pl.BlockSpec((w_out.shape[0], tn), lambda i, j: (0, j))],
        out_specs=pl.BlockSpec((tm, tn), lambda i, j: (i, j)),
        out_shape=jax.ShapeDtypeStruct((m, d), F32),
        compiler_params=pltpu.CompilerParams(dimension_semantics=("parallel", "arbitrary"),
                                             vmem_limit_bytes=VMEM_LIMIT),
        name="outproj",
    )(*ys, x2, w_out)


def _gla_kernel(q_ref, k_ref, v_ref, z_ref, ps_ref, wlr_ref, blr_ref, ng_ref, o_ref, st_ref, *, blk):
    @pl.when(pl.program_id(1) == 0)
    def _():
        st_ref[...] = jnp.zeros_like(st_ref)

    logits = _dot(ps_ref[0].astype(BF16), wlr_ref[...]) + blr_ref[...]
    gates = -_softplus(-logits) * (1.0 / GLA_GATE_NORM)

    gr = GROUP
    cpg = gr // CHUNK
    same, bd_tril, _ = _chunk_block_masks(gr)
    tri_bd = jnp.where(bd_tril, 1.0, 0.0).astype(BF16)
    ones_bd = jnp.where(same, 1.0, 0.0).astype(BF16)
    ng = ng_ref[...]
    zeros_half = jnp.zeros((CHUNK, GLA_DK), BF16)
    states = [st_ref[h] for h in range(N_HEADS)]

    probs = []
    for gi in range(blk // gr):
        rows = slice(gi * gr, (gi + 1) * gr)
        g3 = _split3(gates[rows])
        b = _dot(tri_bd, g3[0]) + _dot(tri_bd, g3[1]) + _dot(tri_bd, g3[2])
        b_end = _dot(ones_bd, g3[0]) + _dot(ones_bd, g3[1]) + _dot(ones_bd, g3[2])
        qf = q_ref[0, rows, :].astype(F32)
        kf = k_ref[0, rows, :].astype(F32)
        q_dec = qf * (GLA_DK ** -0.5) * jnp.exp(b)
        k_inv = kf * jnp.exp(-b)
        k_end = kf * jnp.exp(b_end - b)
        decay = jnp.exp(b_end)
        for h in range(N_HEADS):
            ks = slice(h * GLA_DK, (h + 1) * GLA_DK)
            vs = slice(h * HEAD_DIM, (h + 1) * HEAD_DIM)
            probs.append(dict(h=h, rows=rows, vs=vs, qd=q_dec[:, ks].astype(BF16), ki=k_inv[:, ks].astype(BF16),
                              ke=k_end[:, ks].astype(BF16), v=v_ref[0, rows, vs],
                              d=[decay[ci * CHUNK:ci * CHUNK + 1, ks] for ci in range(cpg)]))
    for p in probs:
        p["a"] = jnp.where(bd_tril, _dot_nt(p["qd"], p["ki"]), 0.0).astype(BF16)
        vt = p["v"].astype(F32).T.astype(BF16)
        p["ut"] = []
        for ci in range(cpg):
            cr = slice(ci * CHUNK, (ci + 1) * CHUNK)
            pair = (ci // 2) * 2 * CHUNK
            ke_c = p["ke"][cr]
            ke_pad = jnp.concatenate([ke_c, zeros_half] if ci % 2 == 0 else [zeros_half, ke_c], axis=0)
            p["ut"].append(_dot(vt[:, pair:pair + 2 * CHUNK], ke_pad))
    for p in probs:
        h = p["h"]
        o_inters = []
        for ci in range(cpg):
            cr = slice(ci * CHUNK, (ci + 1) * CHUNK)
            o_inters.append(_dot_nt(p["qd"][cr], states[h].astype(BF16)))
            states[h] = states[h] * p["d"][ci] + p["ut"][ci]
        p["o"] = jnp.concatenate(o_inters, axis=0)
    for p in probs:
        o = _dot(p["a"], p["v"]) + p["o"]
        zf = z_ref[0, p["rows"], p["vs"]].astype(F32)
        o_ref[0, p["rows"], p["vs"]] = (_rms(o, ng) * _silu(zf)).astype(BF16)
    for h in range(N_HEADS):
        st_ref[h] = states[h]


def _gla(p3, ps3, wlr, blr, ng, *, blk=512):
    bsz, t, _ = p3.shape
    kern = functools.partial(_gla_kernel, blk=blk)
    return pl.pallas_call(
        kern,
        grid=(bsz, t // blk),
        in_specs=[pl.BlockSpec((1, blk, 256), lambda b, i: (b, i, _blk("aq", 256))),
                  pl.BlockSpec((1, blk, 256), lambda b, i: (b, i, _blk("ak", 256))),
                  pl.BlockSpec((1, blk, 512), lambda b, i: (b, i, _blk("av", 512))),
                  pl.BlockSpec((1, blk, 512), lambda b, i: (b, i, _blk("az", 512))),
                  pl.BlockSpec((1, blk, LANES), lambda b, i: (b, i, 0)),
                  pl.BlockSpec((LANES, 256), lambda b, i: (0, 0)),
                  pl.BlockSpec((1, 256), lambda b, i: (0, 0)),
                  pl.BlockSpec((1, HEAD_DIM), lambda b, i: (0, 0))],
        out_specs=pl.BlockSpec((1, blk, GROUP_WIDTH), lambda b, i: (b, i, 0)),
        out_shape=jax.ShapeDtypeStruct((bsz, t, GROUP_WIDTH), BF16),
        scratch_shapes=[pltpu.VMEM((N_HEADS, HEAD_DIM, GLA_DK), F32)],
        compiler_params=pltpu.CompilerParams(dimension_semantics=("parallel", "arbitrary"),
                                             vmem_limit_bytes=VMEM_LIMIT),
        name="gla",
    )(p3, p3, p3, p3, ps3, wlr, blr, ng)


def _swa_kernel(slopes_ref, sinks_ref, q_ref, k_ref, kp_ref, v_ref, vp_ref, z_ref, qg_ref, kg_ref,
                o_ref, *, tq):
    w = SWA_WINDOW
    g = pl.program_id(1)
    i = pl.program_id(2)
    kc = _rms(k_ref[0].astype(F32), kg_ref[...]).astype(BF16)
    kp = _rms(kp_ref[0].astype(F32), kg_ref[...]).astype(BF16)
    row = lax.broadcasted_iota(jnp.int32, (2 * w, 1), 0)
    first = row < w
    slope = jnp.where(first, slopes_ref[2 * g], slopes_ref[2 * g + 1])
    sink = jnp.where(first, sinks_ref[2 * g], sinks_ref[2 * g + 1])
    qi = jnp.bitwise_and(lax.broadcasted_iota(jnp.int32, (2 * w, 2 * w), 0), w - 1)
    kj = lax.broadcasted_iota(jnp.int32, (2 * w, 2 * w), 1)
    dist = qi + w - kj
    in_win = lax.bitcast_convert_type(dist, jnp.uint32) < w
    bias = slope * dist.astype(F32)
    for n in range(tq // w):
        rows = slice(n * w, (n + 1) * w)
        k_prev = kp if n == 0 else kc[(n - 1) * w:n * w]
        v_prev = vp_ref[0] if n == 0 else v_ref[0, (n - 1) * w:n * w, :]
        kb = jnp.concatenate([k_prev, kc[rows]], axis=0)
        vb = jnp.concatenate([v_prev, v_ref[0, rows, :]], axis=0)
        qf = q_ref[0, rows, :].astype(F32)
        qs = jnp.concatenate([qf[:, :HEAD_DIM], qf[:, HEAD_DIM:]], axis=0)
        qs = (_rms(qs, qg_ref[...]) * (HEAD_DIM ** -0.5)).astype(BF16)
        key_pos = (i * (tq // w) + n - 1) * w + kj
        s = jnp.where(in_win, _dot_nt(qs, kb) - bias, -jnp.inf)
        s = jnp.where(key_pos >= 0, s, -jnp.inf)
        mx = jnp.maximum(jnp.max(s, axis=-1, keepdims=True), sink)
        p = jnp.exp(s - mx)
        den = jnp.sum(p, axis=-1, keepdims=True) + jnp.exp(sink - mx)
        o = _dot(p.astype(BF16), vb) / den
        zf = z_ref[0, rows, :].astype(F32)
        o2 = jnp.concatenate([o[:w], o[w:]], axis=1)
        o_ref[0, rows, :] = (o2 * _silu(zf)).astype(BF16)


def _swa(p3, slopes, sinks, qg, kg, *, tq=512):
    bsz, t, _ = p3.shape
    w = SWA_WINDOW
    nb = tq // w
    kern = functools.partial(_swa_kernel, tq=tq)
    smem = pl.BlockSpec(memory_space=pltpu.SMEM)
    prev = lambda base: (lambda b, g, i: (b, jnp.maximum(i * nb - 1, 0), base + g))
    return pl.pallas_call(
        kern,
        grid=(bsz, 2, t // tq),
        in_specs=[smem, smem,
                  pl.BlockSpec((1, tq, 256), lambda b, g, i: (b, i, _blk("bq", 256) + g)),
                  pl.BlockSpec((1, tq, 128), lambda b, g, i: (b, i, _blk("bk", 128) + g)),
                  pl.BlockSpec((1, w, 128), prev(_blk("bk", 128))),
                  pl.BlockSpec((1, tq, 128), lambda b, g, i: (b, i, _blk("bv", 128) + g)),
                  pl.BlockSpec((1, w, 128), prev(_blk("bv", 128))),
                  pl.BlockSpec((1, tq, 256), lambda b, g, i: (b, i, _blk("bz", 256) + g)),
                  pl.BlockSpec((1, HEAD_DIM), lambda b, g, i: (0, 0)),
                  pl.BlockSpec((1, HEAD_DIM), lambda b, g, i: (0, 0))],
        out_specs=pl.BlockSpec((1, tq, 256), lambda b, g, i: (b, i, g)),
        out_shape=jax.ShapeDtypeStruct((bsz, t, GROUP_WIDTH), BF16),
        compiler_params=pltpu.CompilerParams(dimension_semantics=("parallel", "parallel", "arbitrary"),
                                             vmem_limit_bytes=VMEM_LIMIT),
        name="swa",
    )(slopes, sinks, p3, p3, p3, p3, p3, p3, qg, kg)


DIFF_KAUG = 256
DIFF_TK = 512
DIFF_VROWS = 144
LOG2E = 1.4426950408889634


def _diff_prep_kernel(q_ref, k_ref, v_ref, qg_ref, kg_ref, qt_ref, ka_ref, vt_ref, *, blk, slopes):
    i = pl.program_id(1)
    lane = lax.broadcasted_iota(jnp.int32, (blk, HEAD_DIM), 1)
    low = lane < DIFF_DQK
    pos = (i * blk + lax.broadcasted_iota(jnp.int32, (blk, HEAD_DIM), 0)).astype(F32)
    vrow = lax.broadcasted_iota(jnp.int32, (DIFF_VROWS - HEAD_DIM, DIFF_TK), 0)
    ones_rows = jnp.where(vrow == 0, 1.0, 0.0).astype(BF16)

    def halfnorm(x, gain):
        x2 = x * x
        s_lo = jnp.sum(jnp.where(low, x2, 0.0), axis=-1, keepdims=True)
        s_hi = jnp.sum(jnp.where(low, 0.0, x2), axis=-1, keepdims=True)
        ms = jnp.where(low, s_lo, s_hi) * (1.0 / DIFF_DQK)
        return x * lax.rsqrt(ms + NORM_EPS) * gain

    for h in range(N_HEADS):
        cols = slice(h * HEAD_DIM, (h + 1) * HEAD_DIM)
        qn = halfnorm(q_ref[0, :, cols].astype(F32), qg_ref[...]) * (DIFF_DQK ** -0.5 * LOG2E)
        qt_ref[0, h] = qn.T.astype(BF16)
        kn = halfnorm(k_ref[0, :, cols].astype(F32), kg_ref[...])
        ka_ref[0, h, :, 0:HEAD_DIM] = kn.astype(BF16)
        b_hi, b_mid, b_lo = (b.astype(F32) for b in _split3(pos * (slopes[h] * LOG2E)))
        bias = jnp.where(lane == 0, b_hi, jnp.where(lane == 1, b_mid, jnp.where(lane == 2, b_lo, 0.0)))
        ka_ref[0, h, :, HEAD_DIM:DIFF_KAUG] = bias.astype(BF16)
        for c in range(blk // DIFF_TK):
            vt_ref[0, h, c, 0:HEAD_DIM, :] = (
                v_ref[0, c * DIFF_TK:(c + 1) * DIFF_TK, cols].astype(F32).T.astype(BF16))
            vt_ref[0, h, c, HEAD_DIM:DIFF_VROWS, :] = ones_rows


def _diff_prep(p3, qg, kg, slopes, *, blk=512):
    bsz, t, _ = p3.shape
    kern = functools.partial(_diff_prep_kernel, blk=blk, slopes=slopes)
    return pl.pallas_call(
        kern,
        grid=(bsz, t // blk),
        in_specs=[pl.BlockSpec((1, blk, 512), lambda b, i: (b, i, _blk("cq", 512))),
                  pl.BlockSpec((1, blk, 512), lambda b, i: (b, i, _blk("ck", 512))),
                  pl.BlockSpec((1, blk, 512), lambda b, i: (b, i, _blk("cv", 512))),
                  pl.BlockSpec((1, HEAD_DIM), lambda b, i: (0, 0)),
                  pl.BlockSpec((1, HEAD_DIM), lambda b, i: (0, 0))],
        out_specs=[pl.BlockSpec((1, N_HEADS, HEAD_DIM, blk), lambda b, i: (b, 0, 0, i)),
                   pl.BlockSpec((1, N_HEADS, blk, DIFF_KAUG), lambda b, i: (b, 0, i, 0)),
                   pl.BlockSpec((1, N_HEADS, blk // DIFF_TK, DIFF_VROWS, DIFF_TK),
                                lambda b, i: (b, 0, i, 0, 0))],
        out_shape=[jax.ShapeDtypeStruct((bsz, N_HEADS, HEAD_DIM, t), BF16),
                   jax.ShapeDtypeStruct((bsz, N_HEADS, t, DIFF_KAUG), BF16),
                   jax.ShapeDtypeStruct((bsz, N_HEADS, t // DIFF_TK, DIFF_VROWS, DIFF_TK), BF16)],
        compiler_params=pltpu.CompilerParams(dimension_semantics=("parallel", "parallel"),
                                             vmem_limit_bytes=VMEM_LIMIT),
        name="diff_prep",
    )(p3, p3, p3, qg, kg)


def _diff_flash_kernel(qt_ref, ka_ref, vt_ref, z_ref, lam_ref, sg_ref, o_ref,
                       qa_ref, acc_ref, m_ref, sa_ref, sb_ref, ma_ref, mb_ref, *, tq, tk, lambda_init):
    qi = pl.program_id(2)
    hb = qt_ref.shape[1]
    r = lax.broadcasted_iota(jnp.int32, (HEAD_DIM, tq), 0)
    ones3 = jnp.where(r < 3, 1.0, 0.0).astype(BF16)
    for g in range(hb):
        qt = qt_ref[0, g]
        zero = jnp.zeros_like(qt)
        qa_ref[g, 0:HEAD_DIM, 0:tq] = jnp.where(r < DIFF_DQK, qt, zero)
        qa_ref[g, 0:HEAD_DIM, tq:2 * tq] = jnp.where(r < DIFF_DQK, zero, qt)
        qa_ref[g, HEAD_DIM:DIFF_KAUG, 0:tq] = ones3
        qa_ref[g, HEAD_DIM:DIFF_KAUG, tq:2 * tq] = ones3
    m_ref[...] = jnp.full_like(m_ref, -1e30)
    acc_ref[...] = jnp.zeros_like(acc_ref)

    groups = [(g, slice(c * MXU_COLS, (c + 1) * MXU_COLS), c * MXU_COLS)
              for c in range(2 * tq // MXU_COLS) for g in range(hb)]

    def produce(j, s_ref, mt_ref, grp, masked=False):
        g, cols, c0 = grp
        k0 = pl.multiple_of(j * tk, tk)
        s = _dot(ka_ref[0, g, pl.ds(k0, tk), :], qa_ref[g, :, cols])
        if masked:
            key = k0 + lax.broadcasted_iota(jnp.int32, (tk, MXU_COLS), 0)
            col = c0 + lax.broadcasted_iota(jnp.int32, (tk, MXU_COLS), 1)
            s = jnp.where(key <= qi * tq + jnp.bitwise_and(col, tq - 1), s, -jnp.inf)
        s_ref[g, :, cols] = s
        mt_ref[g, :, cols] = jnp.max(s, axis=0, keepdims=True)

    def consume(j, s_ref, mt_ref, grp):
        g, cols, _ = grp
        m_old = m_ref[g, :, cols]
        m_new = jnp.maximum(m_old, mt_ref[g, :, cols])
        alpha = jnp.exp2(m_old - m_new)
        p = jnp.exp2((s_ref[g, :, cols] - m_new).astype(BF16))
        m_ref[g, :, cols] = m_new
        acc_ref[g, :, cols] = alpha * acc_ref[g, :, cols] + _dot(vt_ref[0, g, j], p)

    n = qi
    last = jnp.maximum(n - 1, 0)
    for grp in groups:
        produce(qi, sa_ref, ma_ref, grp, masked=True)
    for grp in groups:
        produce(0, sb_ref, mb_ref, grp)
        consume(qi, sa_ref, ma_ref, grp)

    def body(jj, carry):
        a = 2 * jj
        for grp in groups:
            produce(a + 1, sa_ref, ma_ref, grp)
            consume(a, sb_ref, mb_ref, grp)
        for grp in groups:
            produce(jnp.minimum(a + 2, last), sb_ref, mb_ref, grp)
            consume(a + 1, sa_ref, ma_ref, grp)
        return carry

    lax.fori_loop(0, n // 2, body, 0)

    @pl.when(n % 2 == 1)
    def _():
        for grp in groups:
            consume(n - 1, sb_ref, mb_ref, grp)

    lam = lam_ref[...]
    lam_full = (jnp.exp(jnp.sum(lam[0:1] * lam[1:2], axis=-1, keepdims=True))
                - jnp.exp(jnp.sum(lam[2:3] * lam[3:4], axis=-1, keepdims=True)) + lambda_init)
    for g in range(hb):
        o_all = acc_ref[g, 0:HEAD_DIM, :] / acc_ref[g, HEAD_DIM:HEAD_DIM + 1, :]
        o = o_all[:, 0:tq] - lam_full * o_all[:, tq:2 * tq]
        var = jnp.mean(o * o, axis=0, keepdims=True)
        on = o * lax.rsqrt(var + NORM_EPS) * sg_ref[...] * (1.0 - lambda_init)
        cols = slice(g * HEAD_DIM, (g + 1) * HEAD_DIM)
        zf = z_ref[0, :, cols].astype(F32)
        o_ref[0, :, cols] = (on.T * _silu(zf)).astype(BF16)


def _diff_flash(qt, ka, vt, p3, lam, sg_col, lambda_init, *, tq=512, hb=2):
    bsz, _, _, t = qt.shape
    tk = DIFF_TK
    assert tq == tk and N_HEADS % hb == 0
    kern = functools.partial(_diff_flash_kernel, tq=tq, tk=tk, lambda_init=lambda_init)
    return pl.pallas_call(
        kern,
        grid=(bsz, N_HEADS // hb, t // tq),
        in_specs=[pl.BlockSpec((1, hb, HEAD_DIM, tq), lambda b, h, i: (b, h, 0, i)),
                  pl.BlockSpec((1, hb, t, DIFF_KAUG), lambda b, h, i: (b, h, 0, 0)),
                  pl.BlockSpec((1, hb, t // tk, DIFF_VROWS, tk), lambda b, h, i: (b, h, 0, 0, 0)),
                  pl.BlockSpec((1, tq, hb * HEAD_DIM), lambda b, h, i: (b, i, _blk("cz", hb * HEAD_DIM) + h)),
                  pl.BlockSpec((4, DIFF_DQK), lambda b, h, i: (0, 0)),
                  pl.BlockSpec((HEAD_DIM, 1), lambda b, h, i: (0, 0))],
        out_specs=pl.BlockSpec((1, tq, hb * HEAD_DIM), lambda b, h, i: (b, i, h)),
        out_shape=jax.ShapeDtypeStruct((bsz, t, GROUP_WIDTH), BF16),
        scratch_shapes=[pltpu.VMEM((hb, DIFF_KAUG, 2 * tq), BF16),
                        pltpu.VMEM((hb, DIFF_VROWS, 2 * tq), F32),
                        pltpu.VMEM((hb, 1, 2 * tq), F32),
                        pltpu.VMEM((hb, tk, 2 * tq), F32),
                        pltpu.VMEM((hb, tk, 2 * tq), F32),
                        pltpu.VMEM((hb, 1, 2 * tq), F32),
                        pltpu.VMEM((hb, 1, 2 * tq), F32)],
        compiler_params=pltpu.CompilerParams(dimension_semantics=("parallel", "parallel", "arbitrary"),
                                             vmem_limit_bytes=VMEM_LIMIT),
        name="diff_flash",
    )(qt, ka, vt, p3, lam, sg_col)


GDN_TAIL = 16
assert GROUP % (2 * CHUNK) == 0


def _gdn_kernel(q_ref, k_ref, v_ref, z_ref, ps_ref, cw_ref, alog_ref, dtb_ref, ng_ref, o_ref,
                tail_ref, s_ref, qn_ref, kn_ref, vv_ref, gate_ref, beta_ref, *, blk):
    @pl.when(pl.program_id(1) == 0)
    def _():
        s_ref[...] = jnp.zeros_like(s_ref)
        tail_ref[...] = jnp.zeros_like(tail_ref)

    nb = q_ref.shape[0]
    gr = GROUP
    n_groups = blk // gr
    cw = cw_ref[...]

    def l2n(x):
        return x * lax.rsqrt(jnp.sum(x * x, axis=-1, keepdims=True) + NORM_EPS)

    sr = lax.broadcasted_iota(jnp.int32, (gr, gr), 0)
    sc = lax.broadcasted_iota(jnp.int32, (gr, gr), 1)
    shifts = [jnp.where(sr - sc == k, 1.0, 0.0).astype(BF16) for k in range(1, GDN_CONV)]
    row8 = lax.broadcasted_iota(jnp.int32, (8, MXU_COLS), 0)

    def conv_pieces(bb, gi):
        rows = slice(gi * gr, (gi + 1) * gr)

        def piece(cb):
            cs = slice(cb * MXU_COLS, (cb + 1) * MXU_COLS)
            src = (q_ref, k_ref, v_ref)[cb // 2]
            hs = slice((cb % 2) * MXU_COLS, (cb % 2 + 1) * MXU_COLS)
            xb = src[bb, rows, hs]
            if gi == 0:
                before = tail_ref[bb, :, cs]
            else:
                before = src[bb, gi * gr - GDN_TAIL:gi * gr, hs]
            before = before.astype(F32)[GDN_TAIL - 8:GDN_TAIL]
            y = cw[GDN_CONV - 1:GDN_CONV, cs] * xb.astype(F32)
            corr = jnp.zeros((8, MXU_COLS), F32)
            for k in range(1, GDN_CONV):
                wk = cw[GDN_CONV - 1 - k:GDN_CONV - k, cs]
                y = y + wk * _dot(shifts[k - 1], xb)
                corr = corr + jnp.where(row8 < k, pltpu.roll(before, k, axis=0), 0.0) * wk
            y = jnp.concatenate([y[0:8] + corr, y[8:gr]], axis=0)
            y = _silu(y)
            for e in range(2):
                ys = y[:, e * LANES:(e + 1) * LANES]
                hc = slice(((cb % 2) * 2 + e) * LANES, ((cb % 2) * 2 + e + 1) * LANES)
                if cb < 2:
                    qn_ref[bb, rows, hc] = l2n(ys) * (HEAD_DIM ** -0.5)
                elif cb < 4:
                    kn_ref[bb, rows, hc] = l2n(ys)
                else:
                    vv_ref[bb, rows, hc] = ys

        return [functools.partial(piece, cb) for cb in range(3 * GROUP_WIDTH // MXU_COLS)]

    ps = ps_ref[...]
    gate_ref[...] = -jnp.exp(alog_ref[...]) * _softplus(ps + dtb_ref[...])
    beta_ref[...] = 1.0 / (1.0 + jnp.exp(-ps))

    cpg = gr // CHUNK
    same, bd_tril, bd_strict = _chunk_block_masks(gr)
    tri_bd = jnp.where(bd_tril, 1.0, 0.0).astype(BF16)
    ones_bd = jnp.where(same, 1.0, 0.0).astype(BF16)
    eye = jnp.where(bd_tril, 1.0, 0.0) - jnp.where(bd_strict, 1.0, 0.0)
    ng = ng_ref[...]

    def setup(bb, gi):
        probs = []
        rows = slice(gi * gr, (gi + 1) * gr)
        g3 = _split3(gate_ref[bb, rows, :])
        dec = _dot(tri_bd, g3[0]) + _dot(tri_bd, g3[1]) + _dot(tri_bd, g3[2])
        dec_end = _dot(ones_bd, g3[0]) + _dot(ones_bd, g3[1]) + _dot(ones_bd, g3[2])
        dec_t = dec.T
        e_dec = jnp.exp(dec)
        e_end = jnp.exp(dec_end - dec)
        e_tot = jnp.exp(dec_end)
        beta = beta_ref[bb, rows, :]
        for h in range(N_HEADS):
            cols = slice(h * HEAD_DIM, (h + 1) * HEAD_DIM)
            la = SMALL_A + h
            diff = dec[:, la:la + 1] - dec_t[la:la + 1, :]
            dmask = jnp.where(bd_tril, jnp.exp(jnp.where(bd_tril, diff, 0.0)), 0.0)
            ecol = e_dec[:, la:la + 1]
            bcol = beta[:, SMALL_BETA + h:SMALL_BETA + h + 1]
            kn = kn_ref[bb, rows, cols]
            qn = qn_ref[bb, rows, cols]
            kb = kn * bcol
            kn_b = kn.astype(BF16)
            m = jnp.where(bd_strict, _dot_nt(kb.astype(BF16), kn_b) * dmask, 0.0)
            probs.append(dict(
                bb=bb, rows=rows, cols=cols, tm=eye - m, pw=m,
                qk_b=(_dot_nt(qn.astype(BF16), kn_b) * dmask).astype(BF16),
                rhs=jnp.concatenate([(vv_ref[bb, rows, cols] * bcol).astype(BF16), (kb * ecol).astype(BF16)],
                                    axis=1),
                qd_b=(qn * ecol).astype(BF16),
                ket_b=(kn * e_end[:, la:la + 1]).T.astype(BF16),
                d=[e_tot[ci * CHUNK:ci * CHUNK + 1, la:la + 1] for ci in range(cpg)]))
        return probs

    def inverse_stages(probs):
        def stage():
            for p in probs:
                pw_b = p["pw"].astype(BF16)
                p["pw"] = _dot(pw_b, pw_b)
            for p in probs:
                p["tm"] = p["tm"] + _dot(p["tm"].astype(BF16), p["pw"].astype(BF16))

        def finish():
            for p in probs:
                uw = _dot(p["tm"].astype(BF16), p["rhs"])
                p["u"] = uw[:, 0:HEAD_DIM]
                p["w_b"] = uw[:, HEAD_DIM:2 * HEAD_DIM].astype(BF16)

        return [stage] * 5 + [finish]

    zero_s = jnp.zeros((HEAD_DIM, HEAD_DIM), BF16)
    zero_v = jnp.zeros((CHUNK, 2 * HEAD_DIM), BF16)

    def lanes2(a, b):
        return jnp.concatenate([a, b], axis=1)

    def scan_group(pairs, states):
        v_news = [[] for _ in pairs]
        o_inters = [[] for _ in pairs]
        ops = [dict(u=lanes2(p0["u"], p1["u"]), w=lanes2(p0["w_b"], p1["w_b"]), qd=lanes2(p0["qd_b"], p1["qd_b"]))
               for p0, p1 in pairs]
        for ci in range(cpg):
            cr = slice(ci * CHUNK, (ci + 1) * CHUNK)
            pair = slice((ci // 2) * 2 * CHUNK, (ci // 2 + 1) * 2 * CHUNK)
            for k, (p0, p1) in enumerate(pairs):
                s_b = states[k].astype(BF16)
                bd_s = jnp.concatenate([lanes2(s_b[:, 0:HEAD_DIM], zero_s),
                                        lanes2(zero_s, s_b[:, HEAD_DIM:2 * HEAD_DIM])], axis=0)
                r = _dot(jnp.concatenate([ops[k]["w"][cr], ops[k]["qd"][cr]], axis=0), bd_s)
                v_new_b = (ops[k]["u"][cr] - r[0:CHUNK]).astype(BF16)
                o_inters[k].append(r[CHUNK:2 * CHUNK])
                v_news[k].append(v_new_b)
                v0 = lanes2(v_new_b[:, 0:HEAD_DIM], zero_v[:, 0:HEAD_DIM])
                v1 = lanes2(zero_v[:, 0:HEAD_DIM], v_new_b[:, HEAD_DIM:2 * HEAD_DIM])
                bd_v = jnp.concatenate([v0, zero_v, v1, zero_v] if ci % 2 == 0 else [zero_v, v0, zero_v, v1],
                                       axis=0)
                ket = lanes2(p0["ket_b"][:, pair], p1["ket_b"][:, pair])
                d_row = lanes2(jnp.broadcast_to(p0["d"][ci], (1, HEAD_DIM)),
                               jnp.broadcast_to(p1["d"][ci], (1, HEAD_DIM)))
                states[k] = states[k] * d_row + _dot(ket, bd_v)
        for k, pr in enumerate(pairs):
            o_inter = jnp.concatenate(o_inters[k], axis=0)
            v_all = jnp.concatenate(v_news[k], axis=0)
            for e, p in enumerate(pr):
                hl = slice(e * HEAD_DIM, (e + 1) * HEAD_DIM)
                o = o_inter[:, hl] + _dot(p["qk_b"], v_all[:, hl])
                zf = z_ref[p["bb"], p["rows"], p["cols"]].astype(F32)
                o_ref[p["bb"], p["rows"], p["cols"]] = (_rms(o, ng) * _silu(zf)).astype(BF16)

    for bb in range(nb):
        for gi in range(n_groups):
            for piece in conv_pieces(bb, gi):
                piece()
    probs = [[setup(bb, gi) for bb in range(nb)] for gi in range(n_groups)]
    for stage in inverse_stages([p for per_g in probs for per_b in per_g for p in per_b]):
        stage()
    states = [s_ref[bb, hp] for bb in range(nb) for hp in range(N_HEADS // 2)]
    for gi in range(n_groups):
        scan_group([(per_b[2 * hp], per_b[2 * hp + 1]) for per_b in probs[gi] for hp in range(N_HEADS // 2)],
                   states)
    for bb in range(nb):
        for e, src in enumerate((q_ref, k_ref, v_ref)):
            tail_ref[bb, :, e * GROUP_WIDTH:(e + 1) * GROUP_WIDTH] = src[bb, blk - GDN_TAIL:blk, :]
        for hp in range(N_HEADS // 2):
            s_ref[bb, hp] = states[bb * (N_HEADS // 2) + hp]


def _gdn(p3, ps3, cw, alog_row, dtb_row, ng, *, blk=512, nb=1):
    bsz, t, _ = p3.shape
    kern = functools.partial(_gdn_kernel, blk=blk)
    return pl.pallas_call(
        kern,
        grid=(bsz // nb, t // blk),
        in_specs=[pl.BlockSpec((nb, blk, 512), lambda b, i: (b, i, _blk("dqkv", 512))),
                  pl.BlockSpec((nb, blk, 512), lambda b, i: (b, i, _blk("dqkv", 512) + 1)),
                  pl.BlockSpec((nb, blk, 512), lambda b, i: (b, i, _blk("dqkv", 512) + 2)),
                  pl.BlockSpec((nb, blk, 512), lambda b, i: (b, i, _blk("dz", 512))),
                  pl.BlockSpec((nb, blk, LANES), lambda b, i: (b, i, 0)),
                  pl.BlockSpec((GDN_CONV, 1536), lambda b, i: (0, 0)),
                  pl.BlockSpec((1, LANES), lambda b, i: (0, 0)),
                  pl.BlockSpec((1, LANES), lambda b, i: (0, 0)),
                  pl.BlockSpec((1, HEAD_DIM), lambda b, i: (0, 0))],
        out_specs=pl.BlockSpec((nb, blk, GROUP_WIDTH), lambda b, i: (b, i, 0)),
        out_shape=jax.ShapeDtypeStruct((bsz, t, GROUP_WIDTH), BF16),
        scratch_shapes=[pltpu.VMEM((nb, GDN_TAIL, 3 * GROUP_WIDTH), BF16),
                        pltpu.VMEM((nb, N_HEADS // 2, HEAD_DIM, 2 * HEAD_DIM), F32),
                        pltpu.VMEM((nb, blk, GROUP_WIDTH), F32),
                        pltpu.VMEM((nb, blk, GROUP_WIDTH), F32),
                        pltpu.VMEM((nb, blk, GROUP_WIDTH), F32),
                        pltpu.VMEM((nb, blk, LANES), F32),
                        pltpu.VMEM((nb, blk, LANES), F32)],
        compiler_params=pltpu.CompilerParams(dimension_semantics=("parallel", "arbitrary"),
                                             vmem_limit_bytes=VMEM_LIMIT),
        name="gdn",
    )(p3, p3, p3, p3, ps3, cw, alog_row, dtb_row, ng)


def _alibi_slopes():
    n = 2 * N_HEADS
    s = [2.0 ** (-8.0 * (i + 1) / n) for i in range(n)]
    return s[0::2], s[1::2]


def _pack_w_in(w_in):
    w_b = lax.optimization_barrier(w_in.astype(BF16))

    def ref_range(first, last):
        return w_b[..., _REF_OFF[first][0]:_REF_OFF[last][0] + _REF_OFF[last][1]]

    w_main = jnp.concatenate([ref_range("aq", "av"), ref_range("az", "dqkv"), ref_range("dz", "dz")], axis=-1)
    assert w_main.shape[-1] == MAIN_WIDTH
    small = [ref_range(n, n) for n in ("alr", "dbeta", "da")]
    n_small = sum(s.shape[-1] for s in small)
    w_small = jnp.concatenate(small + [jnp.zeros(w_b.shape[:-1] + (LANES - n_small,), BF16)], axis=-1)
    return w_main, w_small


def _lane_row(vals, start):
    depth, n = vals.shape
    return jnp.pad(vals.astype(F32), ((0, 0), (start, LANES - start - n)))[:, None, :]


def kernel(x, norm_g, w_in, w_out, gla_w_lr2, gla_b_lr2, gla_norm_g, swa_q_norm_g, swa_k_norm_g, swa_sinks,
           diff_q_norm_g, diff_k_norm_g, diff_lambda, diff_subln_g, gdn_conv_w, gdn_A_log, gdn_dt_bias,
           gdn_norm_g):
    bsz, t, d = x.shape
    depth = w_in.shape[0]
    slopes_b, slopes_c = _alibi_slopes()
    w_main, w_small = _pack_w_in(w_in)
    w_out_b = w_out.astype(BF16)
    wlr = jnp.pad(gla_w_lr2, ((0, 0), (0, LANES - GLA_RANK), (0, 0))).astype(BF16)
    alog_row = _lane_row(gdn_A_log, SMALL_A)
    dtb_row = _lane_row(gdn_dt_bias, SMALL_A)
    slopes_b_arr = jnp.asarray(slopes_b, F32)

    x2 = x.reshape(bsz * t, d)
    for l in range(depth):
        lambda_init = 0.8 - 0.6 * math.exp(-0.3 * l)
        p_main, p_small = _inproj(x2, norm_g[l][None, :], w_main[l], w_small[l])
        p3 = p_main.reshape(bsz, t, MAIN_WIDTH)
        ps3 = p_small.reshape(bsz, t, LANES)
        ya = _gla(p3, ps3, wlr[l], gla_b_lr2[l][None, :], gla_norm_g[l][None, :])
        yb = _swa(p3, slopes_b_arr, swa_sinks[l], swa_q_norm_g[l][None, :], swa_k_norm_g[l][None, :])
        qt, ka, vt = _diff_prep(p3, jnp.tile(diff_q_norm_g[l], 2)[None, :],
                                jnp.tile(diff_k_norm_g[l], 2)[None, :], tuple(slopes_c))
        yc = _diff_flash(qt, ka, vt, p3, diff_lambda[l], diff_subln_g[l][:, None], lambda_init)
        yd = _gdn(p3, ps3, gdn_conv_w[l], alog_row[l], dtb_row[l], gdn_norm_g[l][None, :])
        ys = [y.reshape(bsz * t, GROUP_WIDTH) for y in (ya, yb, yc, yd)]
        x2 = _outproj(ys, x2, w_out_b[l])
    return x2.reshape(bsz, t, d)
```

```python
import functools
import math

import numpy as np
import jax
import jax.numpy as jnp
from jax import lax
from jax.experimental import pallas as pl
from jax.experimental.pallas import tpu as pltpu

F32 = jnp.float32
BF16 = jnp.bfloat16

D_MODEL = 2048
N_HEADS = 4
HEAD_DIM = 128
GROUP_WIDTH = 512
GLA_DK = 64
GLA_RANK = 16
GLA_GATE_NORM = 16.0
CHUNK = 64
GROUP = 256
SWA_WINDOW = 128
DIFF_DQK = 64
GDN_CONV = 4
NORM_EPS = 1e-6
LANES = 128
MXU_COLS = 256
VMEM_LIMIT = 56 * 1024 * 1024
INPROJ_CHUNK = 256

_REF_SIZES = (("aq", 256), ("ak", 256), ("av", 512), ("alr", 16), ("az", 512),
              ("bq", 512), ("bk", 256), ("bv", 256), ("bz", 512),
              ("cq", 512), ("ck", 512), ("cv", 512), ("cz", 512),
              ("dqkv", 1536), ("dbeta", 4), ("da", 4), ("dz", 512))
_REF_OFF = {}
_o = 0
for _n, _w in _REF_SIZES:
    _REF_OFF[_n] = (_o, _w)
    _o += _w
PROJ_WIDTH = _o

_MAIN_ORDER = ("aq", "ak", "av", "az", "bq", "bk", "bv", "bz", "cq", "ck", "cv", "cz", "dqkv", "dz")
_MAIN_OFF = {}
_o = 0
for _n in _MAIN_ORDER:
    _MAIN_OFF[_n] = _o
    _o += _REF_OFF[_n][1]
MAIN_WIDTH = _o
SMALL_LR, SMALL_BETA, SMALL_A = 0, 16, 20


def _blk(name, width):
    off = _MAIN_OFF[name]
    assert off % width == 0
    return off // width


def _silu(x):
    return x / (1.0 + jnp.exp(-x))


def _softplus(x):
    return jnp.maximum(x, 0.0) + jnp.log(1.0 + jnp.exp(-jnp.abs(x)))


def _rms(x, gain):
    return x * lax.rsqrt(jnp.mean(x * x, axis=-1, keepdims=True) + NORM_EPS) * gain


def _dot(a, b):
    return jnp.dot(a, b, preferred_element_type=F32)


def _dot_nt(a, b):
    return lax.dot_general(a, b, (((1,), (1,)), ((), ())), preferred_element_type=F32)


def _dot_tn(a, b):
    return lax.dot_general(a, b, (((0,), (0,)), ((), ())), preferred_element_type=F32)


def _split3(x):
    hi = x.astype(BF16)
    r1 = x - hi.astype(F32)
    mid = r1.astype(BF16)
    lo = (r1 - mid.astype(F32)).astype(BF16)
    return hi, mid, lo


def _cumsum_rows(tri, x):
    hi, mid, lo = _split3(x)
    return _dot(tri, hi) + _dot(tri, mid) + _dot(tri, lo)


def _tri_masks(n):
    r = lax.broadcasted_iota(jnp.int32, (n, n), 0)
    c = lax.broadcasted_iota(jnp.int32, (n, n), 1)
    return r >= c, r > c


def _chunk_block_masks(gr):
    r = lax.broadcasted_iota(jnp.int32, (gr, gr), 0)
    c = lax.broadcasted_iota(jnp.int32, (gr, gr), 1)
    u32 = functools.partial(lax.bitcast_convert_type, new_dtype=jnp.uint32)
    r_in = jnp.bitwise_and(r, CHUNK - 1)
    same = jnp.right_shift(r, CHUNK.bit_length() - 1) == jnp.right_shift(c, CHUNK.bit_length() - 1)
    bd_tril = u32(r - c) <= u32(r_in)
    bd_strict = u32(r - c - 1) < u32(r_in)
    return same, bd_tril, bd_strict


def _inproj_kernel(x_ref, g_ref, w_ref, ws_ref, p_ref, ps_ref, h_ref):
    @pl.when(pl.program_id(1) == 0)
    def _():
        tm = x_ref.shape[0]
        for c in range(tm // INPROJ_CHUNK):
            rows = slice(c * INPROJ_CHUNK, (c + 1) * INPROJ_CHUNK)
            h = _rms(x_ref[rows, :], g_ref[...]).astype(BF16)
            h_ref[rows, :] = h
            ps_ref[rows, :] = _dot(h, ws_ref[...])
            p_ref[rows, :] = _dot(h, w_ref[...]).astype(BF16)

    @pl.when(pl.program_id(1) != 0)
    def _():
        p_ref[...] = _dot(h_ref[...], w_ref[...]).astype(BF16)


def _inproj(x2, gain, w_main, w_small, layer, *, tm=1024, tn=1792):
    m, d = x2.shape
    n = w_main.shape[-1]
    return pl.pallas_call(
        _inproj_kernel,
        grid=(m // tm, n // tn),
        in_specs=[pl.BlockSpec((tm, d), lambda i, j: (i, 0)),
                  pl.BlockSpec((1, d), lambda i, j: (0, 0)),
                  pl.BlockSpec((None, d, tn), lambda i, j: (layer, 0, j)),
                  pl.BlockSpec((None, d, LANES), lambda i, j: (layer, 0, 0))],
        out_specs=[pl.BlockSpec((tm, tn), lambda i, j: (i, j)),
                   pl.BlockSpec((tm, LANES), lambda i, j: (i, 0))],
        out_shape=[jax.ShapeDtypeStruct((m, n), BF16), jax.ShapeDtypeStruct((m, LANES), F32)],
        scratch_shapes=[pltpu.VMEM((tm, d), BF16)],
        compiler_params=pltpu.CompilerParams(dimension_semantics=("parallel", "arbitrary"),
                                             vmem_limit_bytes=VMEM_LIMIT),
        name="inproj",
    )(x2, gain, w_main, w_small)


def _outproj_kernel(ya_ref, yb_ref, yc_ref, yd_ref, x_ref, w_ref, o_ref):
    acc = x_ref[...]
    for g, y_ref in enumerate((ya_ref, yb_ref, yc_ref, yd_ref)):
        acc = acc + _dot(y_ref[...], w_ref[g * GROUP_WIDTH:(g + 1) * GROUP_WIDTH, :])
    o_ref[...] = acc


def _outproj(ys, x2, w_out, layer, *, tm=512, tn=D_MODEL):
    m, d = x2.shape
    yspec = pl.BlockSpec((tm, GROUP_WIDTH), lambda i, j: (i, 0))
    return pl.pallas_call(
        _outproj_kernel,
        grid=(m // tm, d // tn),
        in_specs=[yspec, yspec, yspec, yspec,
                  pl.BlockSpec((tm, tn), lambda i, j: (i, j)),
                  pl.BlockSpec((None, w_out.shape[1], tn), lambda i, j: (layer, 0, j))],
        out_specs=pl.BlockSpec((tm, tn), lambda i, j: (i, j)),
        out_shape=jax.ShapeDtypeStruct((m, d), F32),
        compiler_params=pltpu.CompilerParams(dimension_semantics=("parallel", "arbitrary"),
                                             vmem_limit_bytes=VMEM_LIMIT),
        name="outproj",
    )(*ys, x2, w_out)


def _gla_kernel(q_ref, k_ref, v_ref, z_ref, ps_ref, wlr_ref, blr_ref, ng_ref, o_ref, st_ref, *, blk):
    @pl.when(pl.program_id(1) == 0)
    def _():
        st_ref[...] = jnp.zeros_like(st_ref)

    logits = _dot(ps_ref[0].astype(BF16), wlr_ref[...]) + blr_ref[...]
    gates = -_softplus(-logits) * (1.0 / GLA_GATE_NORM)

    gr = GROUP
    cpg = gr // CHUNK
    same, bd_tril, _ = _chunk_block_masks(gr)
    tri_bd = jnp.where(bd_tril, 1.0, 0.0).astype(BF16)
    ones_bd = jnp.where(same, 1.0, 0.0).astype(BF16)
    ng = ng_ref[...]
    zeros_half = jnp.zeros((CHUNK, GLA_DK), BF16)
    states = [st_ref[h] for h in range(N_HEADS)]

    probs = []
    for gi in range(blk // gr):
        rows = slice(gi * gr, (gi + 1) * gr)
        g3 = _split3(gates[rows])
        b = _dot(tri_bd, g3[0]) + _dot(tri_bd, g3[1]) + _dot(tri_bd, g3[2])
        b_end = _dot(ones_bd, g3[0]) + _dot(ones_bd, g3[1]) + _dot(ones_bd, g3[2])
        qf = q_ref[0, rows, :].astype(F32)
        kf = k_ref[0, rows, :].astype(F32)
        q_dec = qf * (GLA_DK ** -0.5) * jnp.exp(b)
        k_inv = kf * jnp.exp(-b)
        k_end = kf * jnp.exp(b_end - b)
        decay = jnp.exp(b_end)
        for h in range(N_HEADS):
            ks = slice(h * GLA_DK, (h + 1) * GLA_DK)
            vs = slice(h * HEAD_DIM, (h + 1) * HEAD_DIM)
            probs.append(dict(h=h, rows=rows, vs=vs, qd=q_dec[:, ks].astype(BF16), ki=k_inv[:, ks].astype(BF16),
                              ke=k_end[:, ks].astype(BF16), v=v_ref[0, rows, vs],
                              d=[decay[ci * CHUNK:ci * CHUNK + 1, ks] for ci in range(cpg)]))
    for p in probs:
        p["a"] = jnp.where(bd_tril, _dot_nt(p["qd"], p["ki"]), 0.0).astype(BF16)
        vt = p["v"].astype(F32).T.astype(BF16)
        p["ut"] = []
        for ci in range(cpg):
            cr = slice(ci * CHUNK, (ci + 1) * CHUNK)
            pair = (ci // 2) * 2 * CHUNK
            ke_c = p["ke"][cr]
            ke_pad = jnp.concatenate([ke_c, zeros_half] if ci % 2 == 0 else [zeros_half, ke_c], axis=0)
            p["ut"].append(_dot(vt[:, pair:pair + 2 * CHUNK], ke_pad))
    for p in probs:
        h = p["h"]
        o_inters = []
        for ci in range(cpg):
            cr = slice(ci * CHUNK, (ci + 1) * CHUNK)
            o_inters.append(_dot_nt(p["qd"][cr], states[h].astype(BF16)))
            states[h] = states[h] * p["d"][ci] + p["ut"][ci]
        p["o"] = jnp.concatenate(o_inters, axis=0)
    for p in probs:
        o = _dot(p["a"], p["v"]) + p["o"]
        zf = z_ref[0, p["rows"], p["vs"]].astype(F32)
        o_ref[0, p["rows"], p["vs"]] = (_rms(o, ng) * _silu(zf)).astype(BF16)
    for h in range(N_HEADS):
        st_ref[h] = states[h]


def _gla(p3, ps3, wlr, blr, ng, *, blk=512):
    bsz, t, _ = p3.shape
    kern = functools.partial(_gla_kernel, blk=blk)
    return pl.pallas_call(
        kern,
        grid=(bsz, t // blk),
        in_specs=[pl.BlockSpec((1, blk, 256), lambda b, i: (b, i, _blk("aq", 256))),
                  pl.BlockSpec((1, blk, 256), lambda b, i: (b, i, _blk("ak", 256))),
                  pl.BlockSpec((1, blk, 512), lambda b, i: (b, i, _blk("av", 512))),
                  pl.BlockSpec((1, blk, 512), lambda b, i: (b, i, _blk("az", 512))),
                  pl.BlockSpec((1, blk, LANES), lambda b, i: (b, i, 0)),
                  pl.BlockSpec((LANES, 256), lambda b, i: (0, 0)),
                  pl.BlockSpec((1, 256), lambda b, i: (0, 0)),
                  pl.BlockSpec((1, HEAD_DIM), lambda b, i: (0, 0))],
        out_specs=pl.BlockSpec((1, blk, GROUP_WIDTH), lambda b, i: (b, i, 0)),
        out_shape=jax.ShapeDtypeStruct((bsz, t, GROUP_WIDTH), BF16),
        scratch_shapes=[pltpu.VMEM((N_HEADS, HEAD_DIM, GLA_DK), F32)],
        compiler_params=pltpu.CompilerParams(dimension_semantics=("parallel", "arbitrary"),
                                             vmem_limit_bytes=VMEM_LIMIT),
        name="gla",
    )(p3, p3, p3, p3, ps3, wlr, blr, ng)


def _swa_kernel(slopes_ref, sinks_ref, q_ref, k_ref, kp_ref, v_ref, vp_ref, z_ref, qg_ref, kg_ref,
                o_ref, *, tq):
    w = SWA_WINDOW
    g = pl.program_id(1)
    i = pl.program_id(2)
    kc = _rms(k_ref[0].astype(F32), kg_ref[...]).astype(BF16)
    kp = _rms(kp_ref[0].astype(F32), kg_ref[...]).astype(BF16)
    row = lax.broadcasted_iota(jnp.int32, (2 * w, 1), 0)
    first = row < w
    slope = jnp.where(first, slopes_ref[2 * g], slopes_ref[2 * g + 1])
    sink = jnp.where(first, sinks_ref[2 * g], sinks_ref[2 * g + 1])
    qi = jnp.bitwise_and(lax.broadcasted_iota(jnp.int32, (2 * w, 2 * w), 0), w - 1)
    kj = lax.broadcasted_iota(jnp.int32, (2 * w, 2 * w), 1)
    dist = qi + w - kj
    in_win = lax.bitcast_convert_type(dist, jnp.uint32) < w
    bias = slope * dist.astype(F32)
    for n in range(tq // w):
        rows = slice(n * w, (n + 1) * w)
        k_prev = kp if n == 0 else kc[(n - 1) * w:n * w]
        v_prev = vp_ref[0] if n == 0 else v_ref[0, (n - 1) * w:n * w, :]
        kb = jnp.concatenate([k_prev, kc[rows]], axis=0)
        vb = jnp.concatenate([v_prev, v_ref[0, rows, :]], axis=0)
        qf = q_ref[0, rows, :].astype(F32)
        qs = jnp.concatenate([qf[:, :HEAD_DIM], qf[:, HEAD_DIM:]], axis=0)
        qs = (_rms(qs, qg_ref[...]) * (HEAD_DIM ** -0.5)).astype(BF16)
        key_pos = (i * (tq // w) + n - 1) * w + kj
        s = jnp.where(in_win, _dot_nt(qs, kb) - bias, -jnp.inf)
        s = jnp.where(key_pos >= 0, s, -jnp.inf)
        mx = jnp.maximum(jnp.max(s, axis=-1, keepdims=True), sink)
        p = jnp.exp(s - mx)
        den = jnp.sum(p, axis=-1, keepdims=True) + jnp.exp(sink - mx)
        o = _dot(p.astype(BF16), vb) / den
        zf = z_ref[0, rows, :].astype(F32)
        o2 = jnp.concatenate([o[:w], o[w:]], axis=1)
        o_ref[0, rows, :] = (o2 * _silu(zf)).astype(BF16)


def _swa(p3, slopes, sinks, qg, kg, *, tq=512):
    bsz, t, _ = p3.shape
    w = SWA_WINDOW
    nb = tq // w
    kern = functools.partial(_swa_kernel, tq=tq)
    smem = pl.BlockSpec(memory_space=pltpu.SMEM)
    prev = lambda base: (lambda b, g, i: (b, jnp.maximum(i * nb - 1, 0), base + g))
    return pl.pallas_call(
        kern,
        grid=(bsz, 2, t // tq),
        in_specs=[smem, smem,
                  pl.BlockSpec((1, tq, 256), lambda b, g, i: (b, i, _blk("bq", 256) + g)),
                  pl.BlockSpec((1, tq, 128), lambda b, g, i: (b, i, _blk("bk", 128) + g)),
                  pl.BlockSpec((1, w, 128), prev(_blk("bk", 128))),
                  pl.BlockSpec((1, tq, 128), lambda b, g, i: (b, i, _blk("bv", 128) + g)),
                  pl.BlockSpec((1, w, 128), prev(_blk("bv", 128))),
                  pl.BlockSpec((1, tq, 256), lambda b, g, i: (b, i, _blk("bz", 256) + g)),
                  pl.BlockSpec((1, HEAD_DIM), lambda b, g, i: (0, 0)),
                  pl.BlockSpec((1, HEAD_DIM), lambda b, g, i: (0, 0))],
        out_specs=pl.BlockSpec((1, tq, 256), lambda b, g, i: (b, i, g)),
        out_shape=jax.ShapeDtypeStruct((bsz, t, GROUP_WIDTH), BF16),
        compiler_params=pltpu.CompilerParams(dimension_semantics=("parallel", "parallel", "arbitrary"),
                                             vmem_limit_bytes=VMEM_LIMIT),
        name="swa",
    )(slopes, sinks, p3, p3, p3, p3, p3, p3, qg, kg)


DIFF_KAUG = 256
DIFF_TK = 512
DIFF_VROWS = 144
LOG2E = 1.4426950408889634


def _diff_prep_kernel(q_ref, k_ref, v_ref, qg_ref, kg_ref, qt_ref, ka_ref, vt_ref, *, blk, slopes):
    i = pl.program_id(1)
    lane = lax.broadcasted_iota(jnp.int32, (blk, HEAD_DIM), 1)
    low = lane < DIFF_DQK
    pos = (i * blk + lax.broadcasted_iota(jnp.int32, (blk, HEAD_DIM), 0)).astype(F32)
    vrow = lax.broadcasted_iota(jnp.int32, (DIFF_VROWS - HEAD_DIM, DIFF_TK), 0)
    ones_rows = jnp.where(vrow == 0, 1.0, 0.0).astype(BF16)

    def halfnorm(x, gain):
        x2 = x * x
        s_lo = jnp.sum(jnp.where(low, x2, 0.0), axis=-1, keepdims=True)
        s_hi = jnp.sum(jnp.where(low, 0.0, x2), axis=-1, keepdims=True)
        ms = jnp.where(low, s_lo, s_hi) * (1.0 / DIFF_DQK)
        return x * lax.rsqrt(ms + NORM_EPS) * gain

    for h in range(N_HEADS):
        cols = slice(h * HEAD_DIM, (h + 1) * HEAD_DIM)
        qn = halfnorm(q_ref[0, :, cols].astype(F32), qg_ref[...]) * (DIFF_DQK ** -0.5 * LOG2E)
        qt_ref[0, h] = qn.T.astype(BF16)
        kn = halfnorm(k_ref[0, :, cols].astype(F32), kg_ref[...])
        ka_ref[0, h, :, 0:HEAD_DIM] = kn.astype(BF16)
        b_hi, b_mid, b_lo = (b.astype(F32) for b in _split3(pos * (slopes[h] * LOG2E)))
        bias = jnp.where(lane == 0, b_hi, jnp.where(lane == 1, b_mid, jnp.where(lane == 2, b_lo, 0.0)))
        ka_ref[0, h, :, HEAD_DIM:DIFF_KAUG] = bias.astype(BF16)
        for c in range(blk // DIFF_TK):
            vt_ref[0, h, c, 0:HEAD_DIM, :] = (
                v_ref[0, c * DIFF_TK:(c + 1) * DIFF_TK, cols].astype(F32).T.astype(BF16))
            vt_ref[0, h, c, HEAD_DIM:DIFF_VROWS, :] = ones_rows


def _diff_prep(p3, qg, kg, slopes, *, blk=512):
    bsz, t, _ = p3.shape
    kern = functools.partial(_diff_prep_kernel, blk=blk, slopes=slopes)
    return pl.pallas_call(
        kern,
        grid=(bsz, t // blk),
        in_specs=[pl.BlockSpec((1, blk, 512), lambda b, i: (b, i, _blk("cq", 512))),
                  pl.BlockSpec((1, blk, 512), lambda b, i: (b, i, _blk("ck", 512))),
                  pl.BlockSpec((1, blk, 512), lambda b, i: (b, i, _blk("cv", 512))),
                  pl.BlockSpec((1, HEAD_DIM), lambda b, i: (0, 0)),
                  pl.BlockSpec((1, HEAD_DIM), lambda b, i: (0, 0))],
        out_specs=[pl.BlockSpec((1, N_HEADS, HEAD_DIM, blk), lambda b, i: (b, 0, 0, i)),
                   pl.BlockSpec((1, N_HEADS, blk, DIFF_KAUG), lambda b, i: (b, 0, i, 0)),
                   pl.BlockSpec((1, N_HEADS, blk // DIFF_TK, DIFF_VROWS, DIFF_TK),
                                lambda b, i: (b, 0, i, 0, 0))],
        out_shape=[jax.ShapeDtypeStruct((bsz, N_HEADS, HEAD_DIM, t), BF16),
                   jax.ShapeDtypeStruct((bsz, N_HEADS, t, DIFF_KAUG), BF16),
                   jax.ShapeDtypeStruct((bsz, N_HEADS, t // DIFF_TK, DIFF_VROWS, DIFF_TK), BF16)],
        compiler_params=pltpu.CompilerParams(dimension_semantics=("parallel", "parallel"),
                                             vmem_limit_bytes=VMEM_LIMIT),
        name="diff_prep",
    )(p3, p3, p3, qg, kg)


def _diff_flash_kernel(qt_ref, ka_ref, vt_ref, z_ref, lam_ref, sg_ref, o_ref,
                       qa_ref, acc_ref, m_ref, sa_ref, sb_ref, ma_ref, mb_ref, *, tq, tk, lambda_init):
    qi = pl.program_id(2)
    hb = qt_ref.shape[1]
    r = lax.broadcasted_iota(jnp.int32, (HEAD_DIM, tq), 0)
    ones3 = jnp.where(r < 3, 1.0, 0.0).astype(BF16)
    for g in range(hb):
        qt = qt_ref[0, g]
        zero = jnp.zeros_like(qt)
        qa_ref[g, 0:HEAD_DIM, 0:tq] = jnp.where(r < DIFF_DQK, qt, zero)
        qa_ref[g, 0:HEAD_DIM, tq:2 * tq] = jnp.where(r < DIFF_DQK, zero, qt)
        qa_ref[g, HEAD_DIM:DIFF_KAUG, 0:tq] = ones3
        qa_ref[g, HEAD_DIM:DIFF_KAUG, tq:2 * tq] = ones3
    m_ref[...] = jnp.full_like(m_ref, -1e30)
    acc_ref[...] = jnp.zeros_like(acc_ref)

    groups = [(g, slice(c * MXU_COLS, (c + 1) * MXU_COLS), c * MXU_COLS)
              for c in range(2 * tq // MXU_COLS) for g in range(hb)]

    def produce(j, s_ref, mt_ref, grp, masked=False):
        g, cols, c0 = grp
        k0 = pl.multiple_of(j * tk, tk)
        s = _dot(ka_ref[0, g, pl.ds(k0, tk), :], qa_ref[g, :, cols])
        if masked:
            key = k0 + lax.broadcasted_iota(jnp.int32, (tk, MXU_COLS), 0)
            col = c0 + lax.broadcasted_iota(jnp.int32, (tk, MXU_COLS), 1)
            s = jnp.where(key <= qi * tq + jnp.bitwise_and(col, tq - 1), s, -jnp.inf)
        s_ref[g, :, cols] = s
        mt_ref[g, :, cols] = jnp.max(s, axis=0, keepdims=True)

    def consume(j, s_ref, mt_ref, grp):
        g, cols, _ = grp
        m_old = m_ref[g, :, cols]
        m_new = jnp.maximum(m_old, mt_ref[g, :, cols])
        alpha = jnp.exp2(m_old - m_new)
        p = jnp.exp2((s_ref[g, :, cols] - m_new).astype(BF16))
        m_ref[g, :, cols] = m_new
        acc_ref[g, :, cols] = alpha * acc_ref[g, :, cols] + _dot(vt_ref[0, g, j], p)

    n = qi
    last = jnp.maximum(n - 1, 0)
    for grp in groups:
        produce(qi, sa_ref, ma_ref, grp, masked=True)
    for grp in groups:
        produce(0, sb_ref, mb_ref, grp)
        consume(qi, sa_ref, ma_ref, grp)

    def body(jj, carry):
        a = 2 * jj
        for grp in groups:
            produce(a + 1, sa_ref, ma_ref, grp)
            consume(a, sb_ref, mb_ref, grp)
        for grp in groups:
            produce(jnp.minimum(a + 2, last), sb_ref, mb_ref, grp)
            consume(a + 1, sa_ref, ma_ref, grp)
        return carry

    lax.fori_loop(0, n // 2, body, 0)

    @pl.when(n % 2 == 1)
    def _():
        for grp in groups:
            consume(n - 1, sb_ref, mb_ref, grp)

    lam = lam_ref[...]
    lam_full = (jnp.exp(jnp.sum(lam[0:1] * lam[1:2], axis=-1, keepdims=True))
                - jnp.exp(jnp.sum(lam[2:3] * lam[3:4], axis=-1, keepdims=True)) + lambda_init)
    for g in range(hb):
        o_all = acc_ref[g, 0:HEAD_DIM, :] / acc_ref[g, HEAD_DIM:HEAD_DIM + 1, :]
        o = o_all[:, 0:tq] - lam_full * o_all[:, tq:2 * tq]
        var = jnp.mean(o * o, axis=0, keepdims=True)
        on = o * lax.rsqrt(var + NORM_EPS) * sg_ref[...] * (1.0 - lambda_init)
        cols = slice(g * HEAD_DIM, (g + 1) * HEAD_DIM)
        zf = z_ref[0, :, cols].astype(F32)
        o_ref[0, :, cols] = (on.T * _silu(zf)).astype(BF16)


def _diff_flash(qt, ka, vt, p3, lam, sg_col, lambda_init, *, tq=512, hb=2):
    bsz, _, _, t = qt.shape
    tk = DIFF_TK
    assert tq == tk and N_HEADS % hb == 0
    kern = functools.partial(_diff_flash_kernel, tq=tq, tk=tk, lambda_init=lambda_init)
    return pl.pallas_call(
        kern,
        grid=(bsz, N_HEADS // hb, t // tq),
        in_specs=[pl.BlockSpec((1, hb, HEAD_DIM, tq), lambda b, h, i: (b, h, 0, i)),
                  pl.BlockSpec((1, hb, t, DIFF_KAUG), lambda b, h, i: (b, h, 0, 0)),
                  pl.BlockSpec((1, hb, t // tk, DIFF_VROWS, tk), lambda b, h, i: (b, h, 0, 0, 0)),
                  pl.BlockSpec((1, tq, hb * HEAD_DIM), lambda b, h, i: (b, i, _blk("cz", hb * HEAD_DIM) + h)),
                  pl.BlockSpec((4, DIFF_DQK), lambda b, h, i: (0, 0)),
                  pl.BlockSpec((HEAD_DIM, 1), lambda b, h, i: (0, 0))],
        out_specs=pl.BlockSpec((1, tq, hb * HEAD_DIM), lambda b, h, i: (b, i, h)),
        out_shape=jax.ShapeDtypeStruct((bsz, t, GROUP_WIDTH), BF16),
        scratch_shapes=[pltpu.VMEM((hb, DIFF_KAUG, 2 * tq), BF16),
                        pltpu.VMEM((hb, DIFF_VROWS, 2 * tq), F32),
                        pltpu.VMEM((hb, 1, 2 * tq), F32),
                        pltpu.VMEM((hb, tk, 2 * tq), F32),
                        pltpu.VMEM((hb, tk, 2 * tq), F32),
                        pltpu.VMEM((hb, 1, 2 * tq), F32),
                        pltpu.VMEM((hb, 1, 2 * tq), F32)],
        compiler_params=pltpu.CompilerParams(dimension_semantics=("parallel", "parallel", "arbitrary"),
                                             vmem_limit_bytes=VMEM_LIMIT),
        name="diff_flash",
    )(qt, ka, vt, p3, lam, sg_col)


GDN_TAIL = 16
assert GROUP % (2 * CHUNK) == 0


def _gdn_kernel(q_ref, k_ref, v_ref, z_ref, ps_ref, cw_ref, alog_ref, dtb_ref, ng_ref, o_ref,
                tail_ref, s_ref, qn_ref, kn_ref, vv_ref, gate_ref, beta_ref, *, blk):
    @pl.when(pl.program_id(1) == 0)
    def _():
        s_ref[...] = jnp.zeros_like(s_ref)
        tail_ref[...] = jnp.zeros_like(tail_ref)

    nb = q_ref.shape[0]
    gr = GROUP
    n_groups = blk // gr
    cw = cw_ref[...]

    def l2n(x):
        return x * lax.rsqrt(jnp.sum(x * x, axis=-1, keepdims=True) + NORM_EPS)

    sr = lax.broadcasted_iota(jnp.int32, (gr, gr), 0)
    sc = lax.broadcasted_iota(jnp.int32, (gr, gr), 1)
    shifts = [jnp.where(sr - sc == k, 1.0, 0.0).astype(BF16) for k in range(1, GDN_CONV)]
    row8 = lax.broadcasted_iota(jnp.int32, (8, MXU_COLS), 0)

    def conv_pieces(bb, gi):
        rows = slice(gi * gr, (gi + 1) * gr)

        def piece(cb):
            cs = slice(cb * MXU_COLS, (cb + 1) * MXU_COLS)
            src = (q_ref, k_ref, v_ref)[cb // 2]
            hs = slice((cb % 2) * MXU_COLS, (cb % 2 + 1) * MXU_COLS)
            xb = src[bb, rows, hs]
            if gi == 0:
                before = tail_ref[bb, :, cs]
            else:
                before = src[bb, gi * gr - GDN_TAIL:gi * gr, hs]
            before = before.astype(F32)[GDN_TAIL - 8:GDN_TAIL]
            y = cw[GDN_CONV - 1:GDN_CONV, cs] * xb.astype(F32)
            corr = jnp.zeros((8, MXU_COLS), F32)
            for k in range(1, GDN_CONV):
                wk = cw[GDN_CONV - 1 - k:GDN_CONV - k, cs]
                y = y + wk * _dot(shifts[k - 1], xb)
                corr = corr + jnp.where(row8 < k, pltpu.roll(before, k, axis=0), 0.0) * wk
            y = jnp.concatenate([y[0:8] + corr, y[8:gr]], axis=0)
            y = _silu(y)
            for e in range(2):
                ys = y[:, e * LANES:(e + 1) * LANES]
                hc = slice(((cb % 2) * 2 + e) * LANES, ((cb % 2) * 2 + e + 1) * LANES)
                if cb < 2:
                    qn_ref[bb, rows, hc] = l2n(ys) * (HEAD_DIM ** -0.5)
                elif cb < 4:
                    kn_ref[bb, rows, hc] = l2n(ys)
                else:
                    vv_ref[bb, rows, hc] = ys

        return [functools.partial(piece, cb) for cb in range(3 * GROUP_WIDTH // MXU_COLS)]

    ps = ps_ref[...]
    gate_ref[...] = -jnp.exp(alog_ref[...]) * _softplus(ps + dtb_ref[...])
    beta_ref[...] = 1.0 / (1.0 + jnp.exp(-ps))

    cpg = gr // CHUNK
    same, bd_tril, bd_strict = _chunk_block_masks(gr)
    tri_bd = jnp.where(bd_tril, 1.0, 0.0).astype(BF16)
    ones_bd = jnp.where(same, 1.0, 0.0).astype(BF16)
    eye = jnp.where(bd_tril, 1.0, 0.0) - jnp.where(bd_strict, 1.0, 0.0)
    ng = ng_ref[...]

    def setup(bb, gi):
        probs = []
        rows = slice(gi * gr, (gi + 1) * gr)
        g3 = _split3(gate_ref[bb, rows, :])
        dec = _dot(tri_bd, g3[0]) + _dot(tri_bd, g3[1]) + _dot(tri_bd, g3[2])
        dec_end = _dot(ones_bd, g3[0]) + _dot(ones_bd, g3[1]) + _dot(ones_bd, g3[2])
        dec_t = dec.T
        e_dec = jnp.exp(dec)
        e_end = jnp.exp(dec_end - dec)
        e_tot = jnp.exp(dec_end)
        beta = beta_ref[bb, rows, :]
        for h in range(N_HEADS):
            cols = slice(h * HEAD_DIM, (h + 1) * HEAD_DIM)
            la = SMALL_A + h
            diff = dec[:, la:la + 1] - dec_t[la:la + 1, :]
            dmask = jnp.where(bd_tril, jnp.exp(jnp.where(bd_tril, diff, 0.0)), 0.0)
            ecol = e_dec[:, la:la + 1]
            bcol = beta[:, SMALL_BETA + h:SMALL_BETA + h + 1]
            kn = kn_ref[bb, rows, cols]
            qn = qn_ref[bb, rows, cols]
            kb = kn * bcol
            kn_b = kn.astype(BF16)
            m = jnp.where(bd_strict, _dot_nt(kb.astype(BF16), kn_b) * dmask, 0.0)
            probs.append(dict(
                bb=bb, rows=rows, cols=cols, tm=eye - m, pw=m,
                qk_b=(_dot_nt(qn.astype(BF16), kn_b) * dmask).astype(BF16),
                rhs=jnp.concatenate([(vv_ref[bb, rows, cols] * bcol).astype(BF16), (kb * ecol).astype(BF16)],
                                    axis=1),
                qd_b=(qn * ecol).astype(BF16),
                ket_b=(kn * e_end[:, la:la + 1]).T.astype(BF16),
                d=[e_tot[ci * CHUNK:ci * CHUNK + 1, la:la + 1] for ci in range(cpg)]))
        return probs

    def inverse_stages(probs):
        def stage():
            for p in probs:
                pw_b = p["pw"].astype(BF16)
                p["pw"] = _dot(pw_b, pw_b)
            for p in probs:
                p["tm"] = p["tm"] + _dot(p["tm"].astype(BF16), p["pw"].astype(BF16))

        def finish():
            for p in probs:
                uw = _dot(p["tm"].astype(BF16), p["rhs"])
                p["u"] = uw[:, 0:HEAD_DIM]
                p["w_b"] = uw[:, HEAD_DIM:2 * HEAD_DIM].astype(BF16)

        return [stage] * 5 + [finish]

    zero_s = jnp.zeros((HEAD_DIM, HEAD_DIM), BF16)
    zero_v = jnp.zeros((CHUNK, 2 * HEAD_DIM), BF16)

    def lanes2(a, b):
        return jnp.concatenate([a, b], axis=1)

    def scan_group(pairs, states):
        v_news = [[] for _ in pairs]
        o_inters = [[] for _ in pairs]
        ops = [dict(u=lanes2(p0["u"], p1["u"]), w=lanes2(p0["w_b"], p1["w_b"]), qd=lanes2(p0["qd_b"], p1["qd_b"]))
               for p0, p1 in pairs]
        for ci in range(cpg):
            cr = slice(ci * CHUNK, (ci + 1) * CHUNK)
            pair = slice((ci // 2) * 2 * CHUNK, (ci // 2 + 1) * 2 * CHUNK)
            for k, (p0, p1) in enumerate(pairs):
                s_b = states[k].astype(BF16)
                bd_s = jnp.concatenate([lanes2(s_b[:, 0:HEAD_DIM], zero_s),
                                        lanes2(zero_s, s_b[:, HEAD_DIM:2 * HEAD_DIM])], axis=0)
                r = _dot(jnp.concatenate([ops[k]["w"][cr], ops[k]["qd"][cr]], axis=0), bd_s)
                v_new_b = (ops[k]["u"][cr] - r[0:CHUNK]).astype(BF16)
                o_inters[k].append(r[CHUNK:2 * CHUNK])
                v_news[k].append(v_new_b)
                v0 = lanes2(v_new_b[:, 0:HEAD_DIM], zero_v[:, 0:HEAD_DIM])
                v1 = lanes2(zero_v[:, 0:HEAD_DIM], v_new_b[:, HEAD_DIM:2 * HEAD_DIM])
                bd_v = jnp.concatenate([v0, zero_v, v1, zero_v] if ci % 2 == 0 else [zero_v, v0, zero_v, v1],
                                       axis=0)
                ket = lanes2(p0["ket_b"][:, pair], p1["ket_b"][:, pair])
                d_row = lanes2(jnp.broadcast_to(p0["d"][ci], (1, HEAD_DIM)),
                               jnp.broadcast_to(p1["d"][ci], (1, HEAD_DIM)))
                states[k] = states[k] * d_row + _dot(ket, bd_v)
        for k, pr in enumerate(pairs):
            o_inter = jnp.concatenate(o_inters[k], axis=0)
            v_all = jnp.concatenate(v_news[k], axis=0)
            for e, p in enumerate(pr):
                hl = slice(e * HEAD_DIM, (e + 1) * HEAD_DIM)
                o = o_inter[:, hl] + _dot(p["qk_b"], v_all[:, hl])
                zf = z_ref[p["bb"], p["rows"], p["cols"]].astype(F32)
                o_ref[p["bb"], p["rows"], p["cols"]] = (_rms(o, ng) * _silu(zf)).astype(BF16)

    for bb in range(nb):
        for gi in range(n_groups):
            for piece in conv_pieces(bb, gi):
                piece()
    probs = [[setup(bb, gi) for bb in range(nb)] for gi in range(n_groups)]
    for stage in inverse_stages([p for per_g in probs for per_b in per_g for p in per_b]):
        stage()
    states = [s_ref[bb, hp] for bb in range(nb) for hp in range(N_HEADS // 2)]
    for gi in range(n_groups):
        scan_group([(per_b[2 * hp], per_b[2 * hp + 1]) for per_b in probs[gi] for hp in range(N_HEADS // 2)],
                   states)
    for bb in range(nb):
        for e, src in enumerate((q_ref, k_ref, v_ref)):
            tail_ref[bb, :, e * GROUP_WIDTH:(e + 1) * GROUP_WIDTH] = src[bb, blk - GDN_TAIL:blk, :]
        for hp in range(N_HEADS // 2):
            s_ref[bb, hp] = states[bb * (N_HEADS // 2) + hp]


def _gdn(p3, ps3, cw, alog_row, dtb_row, ng, *, blk=512, nb=1):
    bsz, t, _ = p3.shape
    kern = functools.partial(_gdn_kernel, blk=blk)
    return pl.pallas_call(
        kern,
        grid=(bsz // nb, t // blk),
        in_specs=[pl.BlockSpec((nb, blk, 512), lambda b, i: (b, i, _blk("dqkv", 512))),
                  pl.BlockSpec((nb, blk, 512), lambda b, i: (b, i, _blk("dqkv", 512) + 1)),
                  pl.BlockSpec((nb, blk, 512), lambda b, i: (b, i, _blk("dqkv", 512) + 2)),
                  pl.BlockSpec((nb, blk, 512), lambda b, i: (b, i, _blk("dz", 512))),
                  pl.BlockSpec((nb, blk, LANES), lambda b, i: (b, i, 0)),
                  pl.BlockSpec((GDN_CONV, 1536), lambda b, i: (0, 0)),
                  pl.BlockSpec((1, LANES), lambda b, i: (0, 0)),
                  pl.BlockSpec((1, LANES), lambda b, i: (0, 0)),
                  pl.BlockSpec((1, HEAD_DIM), lambda b, i: (0, 0))],
        out_specs=pl.BlockSpec((nb, blk, GROUP_WIDTH), lambda b, i: (b, i, 0)),
        out_shape=jax.ShapeDtypeStruct((bsz, t, GROUP_WIDTH), BF16),
        scratch_shapes=[pltpu.VMEM((nb, GDN_TAIL, 3 * GROUP_WIDTH), BF16),
                        pltpu.VMEM((nb, N_HEADS // 2, HEAD_DIM, 2 * HEAD_DIM), F32),
                        pltpu.VMEM((nb, blk, GROUP_WIDTH), F32),
                        pltpu.VMEM((nb, blk, GROUP_WIDTH), F32),
                        pltpu.VMEM((nb, blk, GROUP_WIDTH), F32),
                        pltpu.VMEM((nb, blk, LANES), F32),
                        pltpu.VMEM((nb, blk, LANES), F32)],
        compiler_params=pltpu.CompilerParams(dimension_semantics=("parallel", "arbitrary"),
                                             vmem_limit_bytes=VMEM_LIMIT),
        name="gdn",
    )(p3, p3, p3, p3, ps3, cw, alog_row, dtb_row, ng)


def _alibi_slopes():
    n = 2 * N_HEADS
    s = [2.0 ** (-8.0 * (i + 1) / n) for i in range(n)]
    return s[0::2], s[1::2]


def _pack_w_in(w_in):
    def ref_range(first, last):
        return w_in[..., _REF_OFF[first][0]:_REF_OFF[last][0] + _REF_OFF[last][1]]

    w_main = jnp.concatenate([ref_range("aq", "av"), ref_range("az", "dqkv"), ref_range("dz", "dz")],
                             axis=-1).astype(BF16)
    assert w_main.shape[-1] == MAIN_WIDTH
    small = [ref_range(n, n) for n in ("alr", "dbeta", "da")]
    n_small = sum(s.shape[-1] for s in small)
    w_small = jnp.concatenate(small + [jnp.zeros(w_in.shape[:-1] + (LANES - n_small,), w_in.dtype)],
                              axis=-1).astype(BF16)
    return w_main, w_small


def _lane_row(vals, start):
    depth, n = vals.shape
    return jnp.pad(vals.astype(F32), ((0, 0), (start, LANES - start - n)))[:, None, :]


def kernel(x, norm_g, w_in, w_out, gla_w_lr2, gla_b_lr2, gla_norm_g, swa_q_norm_g, swa_k_norm_g, swa_sinks,
           diff_q_norm_g, diff_k_norm_g, diff_lambda, diff_subln_g, gdn_conv_w, gdn_A_log, gdn_dt_bias,
           gdn_norm_g):
    bsz, t, d = x.shape
    depth = w_in.shape[0]
    slopes_b, slopes_c = _alibi_slopes()
    w_main, w_small = _pack_w_in(w_in)
    w_out_b = w_out.astype(BF16)
    wlr = jnp.pad(gla_w_lr2, ((0, 0), (0, LANES - GLA_RANK), (0, 0))).astype(BF16)
    alog_row = _lane_row(gdn_A_log, SMALL_A)
    dtb_row = _lane_row(gdn_dt_bias, SMALL_A)
    slopes_b_arr = jnp.asarray(slopes_b, F32)

    x2 = x.reshape(bsz * t, d)
    for l in range(depth):
        lambda_init = 0.8 - 0.6 * math.exp(-0.3 * l)
        p_main, p_small = _inproj(x2, norm_g[l][None, :], w_main, w_small, l)
        p3 = p_main.reshape(bsz, t, MAIN_WIDTH)
        ps3 = p_small.reshape(bsz, t, LANES)
        ya = _gla(p3, ps3, wlr[l], gla_b_lr2[l][None, :], gla_norm_g[l][None, :])
        yb = _swa(p3, slopes_b_arr, swa_sinks[l], swa_q_norm_g[l][None, :], swa_k_norm_g[l][None, :])
        qt, ka, vt = _diff_prep(p3, jnp.tile(diff_q_norm_g[l], 2)[None, :],
                                jnp.tile(diff_k_norm_g[l], 2)[None, :], tuple(slopes_c))
        yc = _diff_flash(qt, ka, vt, p3, diff_lambda[l], diff_subln_g[l][:, None], lambda_init)
        yd = _gdn(p3, ps3, gdn_conv_w[l], alog_row[l], dtb_row[l], gdn_norm_g[l][None, :])
        ys = [y.reshape(bsz * t, GROUP_WIDTH) for y in (ya, yb, yc, yd)]
        x2 = _outproj(ys, x2, w_out_b, l)
    return x2.reshape(bsz, t, d)
```

```python
import functools
import math

import jax
import jax.numpy as jnp
from jax import lax
from jax.experimental import pallas as pl
from jax.experimental.pallas import tpu as pltpu

F32 = jnp.float32
BF16 = jnp.bfloat16

D_MODEL = 2048
N_HEADS = 4
HEAD_DIM = 128
GROUP_WIDTH = 512
GLA_DK = 64
GLA_QK = N_HEADS * GLA_DK
GLA_RANK = 16
GLA_GATE_NORM = 16.0
CHUNK = 64
GROUP = 256
SWA_WINDOW = 128
DIFF_DQK = 64
GDN_CONV = 4
NORM_EPS = 1e-6
LANES = 128
MXU_COLS = 256
VMEM_LIMIT = 56 * 1024 * 1024
INPROJ_CHUNK = 256

_REF_SIZES = (("aq", 256), ("ak", 256), ("av", 512), ("alr", 16), ("az", 512),
              ("bq", 512), ("bk", 256), ("bv", 256), ("bz", 512),
              ("cq", 512), ("ck", 512), ("cv", 512), ("cz", 512),
              ("dqkv", 1536), ("dbeta", 4), ("da", 4), ("dz", 512))
_REF_OFF = {}
_o = 0
for _n, _w in _REF_SIZES:
    _REF_OFF[_n] = (_o, _w)
    _o += _w
PROJ_WIDTH = _o

_MAIN_ORDER = ("aq", "ak", "av", "az", "bq", "bk", "bv", "bz", "cq", "ck", "cv", "cz", "dqkv", "dz")
_MAIN_OFF = {}
_o = 0
for _n in _MAIN_ORDER:
    _MAIN_OFF[_n] = _o
    _o += _REF_OFF[_n][1]
MAIN_WIDTH = _o
SMALL_LR, SMALL_BETA, SMALL_A = 0, 16, 20


def _blk(name, width):
    off = _MAIN_OFF[name]
    assert off % width == 0
    return off // width


def _silu(x):
    return x / (1.0 + jnp.exp(-x))


def _softplus(x):
    return jnp.maximum(x, 0.0) + jnp.log(1.0 + jnp.exp(-jnp.abs(x)))


def _rms(x, gain):
    return x * lax.rsqrt(jnp.mean(x * x, axis=-1, keepdims=True) + NORM_EPS) * gain


def _dot(a, b):
    return jnp.dot(a, b, preferred_element_type=F32)


def _dot_nt(a, b):
    return lax.dot_general(a, b, (((1,), (1,)), ((), ())), preferred_element_type=F32)


def _split3(x):
    hi = x.astype(BF16)
    r1 = x - hi.astype(F32)
    mid = r1.astype(BF16)
    lo = (r1 - mid.astype(F32)).astype(BF16)
    return hi, mid, lo


def _chunk_block_masks(gr):
    r = lax.broadcasted_iota(jnp.int32, (gr, gr), 0)
    c = lax.broadcasted_iota(jnp.int32, (gr, gr), 1)
    u32 = functools.partial(lax.bitcast_convert_type, new_dtype=jnp.uint32)
    r_in = jnp.bitwise_and(r, CHUNK - 1)
    same = jnp.right_shift(r, CHUNK.bit_length() - 1) == jnp.right_shift(c, CHUNK.bit_length() - 1)
    bd_tril = u32(r - c) <= u32(r_in)
    bd_strict = u32(r - c - 1) < u32(r_in)
    return same, bd_tril, bd_strict


def _inproj_kernel(x_ref, g_ref, w_ref, ws_ref, p_ref, ps_ref, h_ref):
    @pl.when(pl.program_id(1) == 0)
    def _():
        tm = x_ref.shape[0]
        for c in range(tm // INPROJ_CHUNK):
            rows = slice(c * INPROJ_CHUNK, (c + 1) * INPROJ_CHUNK)
            h = _rms(x_ref[rows, :], g_ref[...]).astype(BF16)
            h_ref[rows, :] = h
            ps_ref[rows, :] = _dot(h, ws_ref[...])
            p_ref[rows, :] = _dot(h, w_ref[...]).astype(BF16)

    @pl.when(pl.program_id(1) != 0)
    def _():
        p_ref[...] = _dot(h_ref[...], w_ref[...]).astype(BF16)


def _inproj(x2, gain, w_main, w_small, layer, *, tm=1024, tn=1792):
    m, d = x2.shape
    n = w_main.shape[-1]
    return pl.pallas_call(
        _inproj_kernel,
        grid=(m // tm, n // tn),
        in_specs=[pl.BlockSpec((tm, d), lambda i, j: (i, 0)),
                  pl.BlockSpec((1, d), lambda i, j: (0, 0)),
                  pl.BlockSpec((None, d, tn), lambda i, j: (layer, 0, j)),
                  pl.BlockSpec((None, d, LANES), lambda i, j: (layer, 0, 0))],
        out_specs=[pl.BlockSpec((tm, tn), lambda i, j: (i, j)),
                   pl.BlockSpec((tm, LANES), lambda i, j: (i, 0))],
        out_shape=[jax.ShapeDtypeStruct((m, n), BF16), jax.ShapeDtypeStruct((m, LANES), F32)],
        scratch_shapes=[pltpu.VMEM((tm, d), BF16)],
        compiler_params=pltpu.CompilerParams(dimension_semantics=("parallel", "arbitrary"),
                                             vmem_limit_bytes=VMEM_LIMIT),
        name="inproj",
    )(x2, gain, w_main, w_small)


def _outproj_kernel(ya_ref, yb_ref, yc_ref, yd_ref, x_ref, w_ref, o_ref):
    acc = x_ref[...]
    for g, y_ref in enumerate((ya_ref, yb_ref, yc_ref, yd_ref)):
        acc = acc + _dot(y_ref[...], w_ref[g * GROUP_WIDTH:(g + 1) * GROUP_WIDTH, :])
    o_ref[...] = acc


def _outproj(ys, x2, w_out, layer, *, tm=512, tn=D_MODEL):
    m, d = x2.shape
    yspec = pl.BlockSpec((tm, GROUP_WIDTH), lambda i, j: (i, 0))
    return pl.pallas_call(
        _outproj_kernel,
        grid=(m // tm, d // tn),
        in_specs=[yspec, yspec, yspec, yspec,
                  pl.BlockSpec((tm, tn), lambda i, j: (i, j)),
                  pl.BlockSpec((None, w_out.shape[1], tn), lambda i, j: (layer, 0, j))],
        out_specs=pl.BlockSpec((tm, tn), lambda i, j: (i, j)),
        out_shape=jax.ShapeDtypeStruct((m, d), F32),
        compiler_params=pltpu.CompilerParams(dimension_semantics=("parallel", "arbitrary"),
                                             vmem_limit_bytes=VMEM_LIMIT),
        name="outproj",
    )(*ys, x2, w_out)


def _gla_kernel(q_ref, k_ref, v_ref, z_ref, ps_ref, wlr_ref, blr_ref, ng_ref, o_ref, st_ref, *, blk):
    @pl.when(pl.program_id(1) == 0)
    def _():
        st_ref[...] = jnp.zeros_like(st_ref)

    logits = _dot(ps_ref[0].astype(BF16), wlr_ref[...]) + blr_ref[...]
    gates = -_softplus(-logits) * (1.0 / GLA_GATE_NORM)

    gr = GROUP
    cpg = gr // CHUNK
    same, bd_tril, _ = _chunk_block_masks(gr)
    tri_bd = jnp.where(bd_tril, 1.0, 0.0).astype(BF16)
    ones_bd = jnp.where(same, 1.0, 0.0).astype(BF16)
    ng = ng_ref[...]
    zeros_half = jnp.zeros((CHUNK, GLA_DK), BF16)
    states = [st_ref[h] for h in range(N_HEADS)]

    probs = []
    for gi in range(blk // gr):
        rows = slice(gi * gr, (gi + 1) * gr)
        g3 = _split3(gates[rows])
        b = _dot(tri_bd, g3[0]) + _dot(tri_bd, g3[1]) + _dot(tri_bd, g3[2])
        b_end = _dot(ones_bd, g3[0]) + _dot(ones_bd, g3[1]) + _dot(ones_bd, g3[2])
        qf = q_ref[0, rows, :].astype(F32)
        kf = k_ref[0, rows, :].astype(F32)
        q_dec = qf * (GLA_DK ** -0.5) * jnp.exp(b)
        k_inv = kf * jnp.exp(-b)
        k_end = kf * jnp.exp(b_end - b)
        decay = jnp.exp(b_end)
        for h in range(N_HEADS):
            ks = slice(h * GLA_DK, (h + 1) * GLA_DK)
            vs = slice(h * HEAD_DIM, (h + 1) * HEAD_DIM)
            probs.append(dict(h=h, rows=rows, vs=vs, qd=q_dec[:, ks].astype(BF16), ki=k_inv[:, ks].astype(BF16),
                              ke=k_end[:, ks].astype(BF16), v=v_ref[0, rows, vs],
                              d=[decay[ci * CHUNK:ci * CHUNK + 1, ks] for ci in range(cpg)]))
    for p in probs:
        p["a"] = jnp.where(bd_tril, _dot_nt(p["qd"], p["ki"]), 0.0).astype(BF16)
        vt = p["v"].astype(F32).T.astype(BF16)
        p["ut"] = []
        for ci in range(cpg):
            cr = slice(ci * CHUNK, (ci + 1) * CHUNK)
            pair = (ci // 2) * 2 * CHUNK
            ke_c = p["ke"][cr]
            ke_pad = jnp.concatenate([ke_c, zeros_half] if ci % 2 == 0 else [zeros_half, ke_c], axis=0)
            p["ut"].append(_dot(vt[:, pair:pair + 2 * CHUNK], ke_pad))
    for p in probs:
        h = p["h"]
        o_inters = []
        for ci in range(cpg):
            cr = slice(ci * CHUNK, (ci + 1) * CHUNK)
            o_inters.append(_dot_nt(p["qd"][cr], states[h].astype(BF16)))
            states[h] = states[h] * p["d"][ci] + p["ut"][ci]
        p["o"] = jnp.concatenate(o_inters, axis=0)
    for p in probs:
        o = _dot(p["a"], p["v"]) + p["o"]
        zf = z_ref[0, p["rows"], p["vs"]].astype(F32)
        o_ref[0, p["rows"], p["vs"]] = (_rms(o, ng) * _silu(zf)).astype(BF16)
    for h in range(N_HEADS):
        st_ref[h] = states[h]


def _gla(p3, ps3, wlr, blr, ng, *, blk=512):
    bsz, t, _ = p3.shape
    kern = functools.partial(_gla_kernel, blk=blk)
    return pl.pallas_call(
        kern,
        grid=(bsz, t // blk),
        in_specs=[pl.BlockSpec((1, blk, GLA_QK), lambda b, i: (b, i, _blk("aq", GLA_QK))),
                  pl.BlockSpec((1, blk, GLA_QK), lambda b, i: (b, i, _blk("ak", GLA_QK))),
                  pl.BlockSpec((1, blk, GROUP_WIDTH), lambda b, i: (b, i, _blk("av", GROUP_WIDTH))),
                  pl.BlockSpec((1, blk, GROUP_WIDTH), lambda b, i: (b, i, _blk("az", GROUP_WIDTH))),
                  pl.BlockSpec((1, blk, LANES), lambda b, i: (b, i, 0)),
                  pl.BlockSpec((LANES, GLA_QK), lambda b, i: (0, 0)),
                  pl.BlockSpec((1, GLA_QK), lambda b, i: (0, 0)),
                  pl.BlockSpec((1, HEAD_DIM), lambda b, i: (0, 0))],
        out_specs=pl.BlockSpec((1, blk, GROUP_WIDTH), lambda b, i: (b, i, 0)),
        out_shape=jax.ShapeDtypeStruct((bsz, t, GROUP_WIDTH), BF16),
        scratch_shapes=[pltpu.VMEM((N_HEADS, HEAD_DIM, GLA_DK), F32)],
        compiler_params=pltpu.CompilerParams(dimension_semantics=("parallel", "arbitrary"),
                                             vmem_limit_bytes=VMEM_LIMIT),
        name="gla",
    )(p3, p3, p3, p3, ps3, wlr, blr, ng)


def _swa_kernel(slopes_ref, sinks_ref, q_ref, k_ref, kp_ref, v_ref, vp_ref, z_ref, qg_ref, kg_ref,
                o_ref, *, tq):
    w = SWA_WINDOW
    g = pl.program_id(1)
    i = pl.program_id(2)
    kc = _rms(k_ref[0].astype(F32), kg_ref[...]).astype(BF16)
    kp = _rms(kp_ref[0].astype(F32), kg_ref[...]).astype(BF16)
    row = lax.broadcasted_iota(jnp.int32, (2 * w, 1), 0)
    first = row < w
    slope = jnp.where(first, slopes_ref[2 * g], slopes_ref[2 * g + 1])
    sink = jnp.where(first, sinks_ref[2 * g], sinks_ref[2 * g + 1])
    qi = jnp.bitwise_and(lax.broadcasted_iota(jnp.int32, (2 * w, 2 * w), 0), w - 1)
    kj = lax.broadcasted_iota(jnp.int32, (2 * w, 2 * w), 1)
    dist = qi + w - kj
    in_win = lax.bitcast_convert_type(dist, jnp.uint32) < w
    bias = slope * dist.astype(F32)
    for n in range(tq // w):
        rows = slice(n * w, (n + 1) * w)
        k_prev = kp if n == 0 else kc[(n - 1) * w:n * w]
        v_prev = vp_ref[0] if n == 0 else v_ref[0, (n - 1) * w:n * w, :]
        kb = jnp.concatenate([k_prev, kc[rows]], axis=0)
        vb = jnp.concatenate([v_prev, v_ref[0, rows, :]], axis=0)
        qf = q_ref[0, rows, :].astype(F32)
        qs = jnp.concatenate([qf[:, :HEAD_DIM], qf[:, HEAD_DIM:]], axis=0)
        qs = (_rms(qs, qg_ref[...]) * (HEAD_DIM ** -0.5)).astype(BF16)
        key_pos = (i * (tq // w) + n - 1) * w + kj
        s = jnp.where(in_win, _dot_nt(qs, kb) - bias, -jnp.inf)
        s = jnp.where(key_pos >= 0, s, -jnp.inf)
        mx = jnp.maximum(jnp.max(s, axis=-1, keepdims=True), sink)
        p = jnp.exp(s - mx)
        den = jnp.sum(p, axis=-1, keepdims=True) + jnp.exp(sink - mx)
        o = _dot(p.astype(BF16), vb) / den
        zf = z_ref[0, rows, :].astype(F32)
        o2 = jnp.concatenate([o[:w], o[w:]], axis=1)
        o_ref[0, rows, :] = (o2 * _silu(zf)).astype(BF16)


def _swa(p3, slopes, sinks, qg, kg, *, tq=1024):
    bsz, t, _ = p3.shape
    w = SWA_WINDOW
    nb = tq // w
    qw = 2 * HEAD_DIM
    kern = functools.partial(_swa_kernel, tq=tq)
    smem = pl.BlockSpec(memory_space=pltpu.SMEM)
    prev = lambda base: (lambda b, g, i: (b, jnp.maximum(i * nb - 1, 0), base + g))
    return pl.pallas_call(
        kern,
        grid=(bsz, N_HEADS // 2, t // tq),
        in_specs=[smem, smem,
                  pl.BlockSpec((1, tq, qw), lambda b, g, i: (b, i, _blk("bq", qw) + g)),
                  pl.BlockSpec((1, tq, HEAD_DIM), lambda b, g, i: (b, i, _blk("bk", HEAD_DIM) + g)),
                  pl.BlockSpec((1, w, HEAD_DIM), prev(_blk("bk", HEAD_DIM))),
                  pl.BlockSpec((1, tq, HEAD_DIM), lambda b, g, i: (b, i, _blk("bv", HEAD_DIM) + g)),
                  pl.BlockSpec((1, w, HEAD_DIM), prev(_blk("bv", HEAD_DIM))),
                  pl.BlockSpec((1, tq, qw), lambda b, g, i: (b, i, _blk("bz", qw) + g)),
                  pl.BlockSpec((1, HEAD_DIM), lambda b, g, i: (0, 0)),
                  pl.BlockSpec((1, HEAD_DIM), lambda b, g, i: (0, 0))],
        out_specs=pl.BlockSpec((1, tq, qw), lambda b, g, i: (b, i, g)),
        out_shape=jax.ShapeDtypeStruct((bsz, t, GROUP_WIDTH), BF16),
        compiler_params=pltpu.CompilerParams(dimension_semantics=("parallel", "parallel", "arbitrary"),
                                             vmem_limit_bytes=VMEM_LIMIT),
        name="swa",
    )(slopes, sinks, p3, p3, p3, p3, p3, p3, qg, kg)


DIFF_KAUG = 256
DIFF_TK = 512
DIFF_VROWS = 144
LOG2E = 1.4426950408889634


def _diff_prep_kernel(q_ref, k_ref, v_ref, qg_ref, kg_ref, qt_ref, ka_ref, vt_ref, *, blk, slopes):
    i = pl.program_id(1)
    lane = lax.broadcasted_iota(jnp.int32, (blk, HEAD_DIM), 1)
    low = lane < DIFF_DQK
    pos = (i * blk + lax.broadcasted_iota(jnp.int32, (blk, HEAD_DIM), 0)).astype(F32)
    vrow = lax.broadcasted_iota(jnp.int32, (DIFF_VROWS - HEAD_DIM, DIFF_TK), 0)
    ones_rows = jnp.where(vrow == 0, 1.0, 0.0).astype(BF16)

    def halfnorm(x, gain):
        x2 = x * x
        s_lo = jnp.sum(jnp.where(low, x2, 0.0), axis=-1, keepdims=True)
        s_hi = jnp.sum(jnp.where(low, 0.0, x2), axis=-1, keepdims=True)
        ms = jnp.where(low, s_lo, s_hi) * (1.0 / DIFF_DQK)
        return x * lax.rsqrt(ms + NORM_EPS) * gain

    for h in range(N_HEADS):
        cols = slice(h * HEAD_DIM, (h + 1) * HEAD_DIM)
        qn = halfnorm(q_ref[0, :, cols].astype(F32), qg_ref[...]) * (DIFF_DQK ** -0.5 * LOG2E)
        qt_ref[0, h] = qn.T.astype(BF16)
        kn = halfnorm(k_ref[0, :, cols].astype(F32), kg_ref[...])
        ka_ref[0, h, :, 0:HEAD_DIM] = kn.astype(BF16)
        b_hi, b_mid, b_lo = (b.astype(F32) for b in _split3(pos * (slopes[h] * LOG2E)))
        bias = jnp.where(lane == 0, b_hi, jnp.where(lane == 1, b_mid, jnp.where(lane == 2, b_lo, 0.0)))
        ka_ref[0, h, :, HEAD_DIM:DIFF_KAUG] = bias.astype(BF16)
        for c in range(blk // DIFF_TK):
            vt_ref[0, h, c, 0:HEAD_DIM, :] = (
                v_ref[0, c * DIFF_TK:(c + 1) * DIFF_TK, cols].astype(F32).T.astype(BF16))
            vt_ref[0, h, c, HEAD_DIM:DIFF_VROWS, :] = ones_rows


def _diff_prep(p3, qg, kg, slopes, *, blk=512):
    bsz, t, _ = p3.shape
    kern = functools.partial(_diff_prep_kernel, blk=blk, slopes=slopes)
    return pl.pallas_call(
        kern,
        grid=(bsz, t // blk),
        in_specs=[pl.BlockSpec((1, blk, GROUP_WIDTH), lambda b, i: (b, i, _blk("cq", GROUP_WIDTH))),
                  pl.BlockSpec((1, blk, GROUP_WIDTH), lambda b, i: (b, i, _blk("ck", GROUP_WIDTH))),
                  pl.BlockSpec((1, blk, GROUP_WIDTH), lambda b, i: (b, i, _blk("cv", GROUP_WIDTH))),
                  pl.BlockSpec((1, HEAD_DIM), lambda b, i: (0, 0)),
                  pl.BlockSpec((1, HEAD_DIM), lambda b, i: (0, 0))],
        out_specs=[pl.BlockSpec((1, N_HEADS, HEAD_DIM, blk), lambda b, i: (b, 0, 0, i)),
                   pl.BlockSpec((1, N_HEADS, blk, DIFF_KAUG), lambda b, i: (b, 0, i, 0)),
                   pl.BlockSpec((1, N_HEADS, blk // DIFF_TK, DIFF_VROWS, DIFF_TK),
                                lambda b, i: (b, 0, i, 0, 0))],
        out_shape=[jax.ShapeDtypeStruct((bsz, N_HEADS, HEAD_DIM, t), BF16),
                   jax.ShapeDtypeStruct((bsz, N_HEADS, t, DIFF_KAUG), BF16),
                   jax.ShapeDtypeStruct((bsz, N_HEADS, t // DIFF_TK, DIFF_VROWS, DIFF_TK), BF16)],
        compiler_params=pltpu.CompilerParams(dimension_semantics=("parallel", "parallel"),
                                             vmem_limit_bytes=VMEM_LIMIT),
        name="diff_prep",
    )(p3, p3, p3, qg, kg)


def _diff_flash_kernel(qt_ref, ka_ref, vt_ref, z_ref, lam_ref, sg_ref, o_ref,
                       qa_ref, acc_ref, m_ref, sa_ref, sb_ref, ma_ref, mb_ref, *, tq, tk, lambda_init):
    qi = pl.program_id(2)
    hb = qt_ref.shape[1]
    r = lax.broadcasted_iota(jnp.int32, (HEAD_DIM, tq), 0)
    ones3 = jnp.where(r < 3, 1.0, 0.0).astype(BF16)
    for g in range(hb):
        qt = qt_ref[0, g]
        zero = jnp.zeros_like(qt)
        qa_ref[g, 0:HEAD_DIM, 0:tq] = jnp.where(r < DIFF_DQK, qt, zero)
        qa_ref[g, 0:HEAD_DIM, tq:2 * tq] = jnp.where(r < DIFF_DQK, zero, qt)
        qa_ref[g, HEAD_DIM:DIFF_KAUG, 0:tq] = ones3
        qa_ref[g, HEAD_DIM:DIFF_KAUG, tq:2 * tq] = ones3
    m_ref[...] = jnp.full_like(m_ref, -1e30)
    acc_ref[...] = jnp.zeros_like(acc_ref)

    groups = [(g, slice(c * MXU_COLS, (c + 1) * MXU_COLS), c * MXU_COLS)
              for c in range(2 * tq // MXU_COLS) for g in range(hb)]

    def produce(j, s_ref, mt_ref, grp, masked=False):
        g, cols, c0 = grp
        k0 = pl.multiple_of(j * tk, tk)
        s = _dot(ka_ref[0, g, pl.ds(k0, tk), :], qa_ref[g, :, cols])
        if masked:
            key = k0 + lax.broadcasted_iota(jnp.int32, (tk, MXU_COLS), 0)
            col = c0 + lax.broadcasted_iota(jnp.int32, (tk, MXU_COLS), 1)
            s = jnp.where(key <= qi * tq + jnp.bitwise_and(col, tq - 1), s, -jnp.inf)
        s_ref[g, :, cols] = s
        mt_ref[g, :, cols] = jnp.max(s, axis=0, keepdims=True)

    def consume(j, s_ref, mt_ref, grp):
        g, cols, _ = grp
        m_old = m_ref[g, :, cols]
        m_new = jnp.maximum(m_old, mt_ref[g, :, cols])
        alpha = jnp.exp2(m_old - m_new)
        p = jnp.exp2((s_ref[g, :, cols] - m_new).astype(BF16))
        m_ref[g, :, cols] = m_new
        acc_ref[g, :, cols] = alpha * acc_ref[g, :, cols] + _dot(vt_ref[0, g, j], p)

    n = qi
    last = jnp.maximum(n - 1, 0)
    for grp in groups:
        produce(qi, sa_ref, ma_ref, grp, masked=True)
    for grp in groups:
        produce(0, sb_ref, mb_ref, grp)
        consume(qi, sa_ref, ma_ref, grp)

    def body(jj, carry):
        a = 2 * jj
        for grp in groups:
            produce(a + 1, sa_ref, ma_ref, grp)
            consume(a, sb_ref, mb_ref, grp)
        for grp in groups:
            produce(jnp.minimum(a + 2, last), sb_ref, mb_ref, grp)
            consume(a + 1, sa_ref, ma_ref, grp)
        return carry

    lax.fori_loop(0, n // 2, body, 0)

    @pl.when(n % 2 == 1)
    def _():
        for grp in groups:
            consume(n - 1, sb_ref, mb_ref, grp)

    lam = lam_ref[...]
    lam_full = (jnp.exp(jnp.sum(lam[0:1] * lam[1:2], axis=-1, keepdims=True))
                - jnp.exp(jnp.sum(lam[2:3] * lam[3:4], axis=-1, keepdims=True)) + lambda_init)
    for g in range(hb):
        o_all = acc_ref[g, 0:HEAD_DIM, :] * (1.0 / acc_ref[g, HEAD_DIM:HEAD_DIM + 1, :])
        o = o_all[:, 0:tq] - lam_full * o_all[:, tq:2 * tq]
        var = jnp.mean(o * o, axis=0, keepdims=True)
        on = o * lax.rsqrt(var + NORM_EPS) * sg_ref[...] * (1.0 - lambda_init)
        cols = slice(g * HEAD_DIM, (g + 1) * HEAD_DIM)
        zf = z_ref[0, :, cols].astype(F32)
        o_ref[0, :, cols] = (on.T * _silu(zf)).astype(BF16)


def _diff_flash(qt, ka, vt, p3, lam, sg_col, lambda_init, *, tq=512, hb=2):
    bsz, _, _, t = qt.shape
    tk = DIFF_TK
    assert tq == tk and N_HEADS % hb == 0
    kern = functools.partial(_diff_flash_kernel, tq=tq, tk=tk, lambda_init=lambda_init)
    return pl.pallas_call(
        kern,
        grid=(bsz, N_HEADS // hb, t // tq),
        in_specs=[pl.BlockSpec((1, hb, HEAD_DIM, tq), lambda b, h, i: (b, h, 0, i)),
                  pl.BlockSpec((1, hb, t, DIFF_KAUG), lambda b, h, i: (b, h, 0, 0)),
                  pl.BlockSpec((1, hb, t // tk, DIFF_VROWS, tk), lambda b, h, i: (b, h, 0, 0, 0)),
                  pl.BlockSpec((1, tq, hb * HEAD_DIM), lambda b, h, i: (b, i, _blk("cz", hb * HEAD_DIM) + h)),
                  pl.BlockSpec((4, DIFF_DQK), lambda b, h, i: (0, 0)),
                  pl.BlockSpec((HEAD_DIM, 1), lambda b, h, i: (0, 0))],
        out_specs=pl.BlockSpec((1, tq, hb * HEAD_DIM), lambda b, h, i: (b, i, h)),
        out_shape=jax.ShapeDtypeStruct((bsz, t, GROUP_WIDTH), BF16),
        scratch_shapes=[pltpu.VMEM((hb, DIFF_KAUG, 2 * tq), BF16),
                        pltpu.VMEM((hb, DIFF_VROWS, 2 * tq), F32),
                        pltpu.VMEM((hb, 1, 2 * tq), F32),
                        pltpu.VMEM((hb, tk, 2 * tq), F32),
                        pltpu.VMEM((hb, tk, 2 * tq), F32),
                        pltpu.VMEM((hb, 1, 2 * tq), F32),
                        pltpu.VMEM((hb, 1, 2 * tq), F32)],
        compiler_params=pltpu.CompilerParams(dimension_semantics=("parallel", "parallel", "arbitrary"),
                                             vmem_limit_bytes=VMEM_LIMIT),
        name="diff_flash",
    )(qt, ka, vt, p3, lam, sg_col)


GDN_TAIL = 16
assert GROUP % (2 * CHUNK) == 0


def _gdn_kernel(q_ref, k_ref, v_ref, z_ref, ps_ref, cw_ref, alog_ref, dtb_ref, ng_ref, o_ref,
                tail_ref, s_ref, qn_ref, kn_ref, vv_ref, gate_ref, beta_ref, *, blk):
    @pl.when(pl.program_id(1) == 0)
    def _():
        s_ref[...] = jnp.zeros_like(s_ref)
        tail_ref[...] = jnp.zeros_like(tail_ref)

    nb = q_ref.shape[0]
    gr = GROUP
    n_groups = blk // gr
    cw = cw_ref[...]

    def l2n(x):
        return x * lax.rsqrt(jnp.sum(x * x, axis=-1, keepdims=True) + NORM_EPS)

    sr = lax.broadcasted_iota(jnp.int32, (gr, gr), 0)
    sc = lax.broadcasted_iota(jnp.int32, (gr, gr), 1)
    shifts = [jnp.where(sr - sc == k, 1.0, 0.0).astype(BF16) for k in range(1, GDN_CONV)]
    row8 = lax.broadcasted_iota(jnp.int32, (8, MXU_COLS), 0)

    def conv_pieces(bb, gi):
        rows = slice(gi * gr, (gi + 1) * gr)

        def piece(cb):
            cs = slice(cb * MXU_COLS, (cb + 1) * MXU_COLS)
            src = (q_ref, k_ref, v_ref)[cb // 2]
            hs = slice((cb % 2) * MXU_COLS, (cb % 2 + 1) * MXU_COLS)
            xb = src[bb, rows, hs]
            if gi == 0:
                before = tail_ref[bb, :, cs]
            else:
                before = src[bb, gi * gr - GDN_TAIL:gi * gr, hs]
            before = before.astype(F32)[GDN_TAIL - 8:GDN_TAIL]
            y = cw[GDN_CONV - 1:GDN_CONV, cs] * xb.astype(F32)
            corr = jnp.zeros((8, MXU_COLS), F32)
            for k in range(1, GDN_CONV):
                wk = cw[GDN_CONV - 1 - k:GDN_CONV - k, cs]
                y = y + wk * _dot(shifts[k - 1], xb)
                corr = corr + jnp.where(row8 < k, pltpu.roll(before, k, axis=0), 0.0) * wk
            y = jnp.concatenate([y[0:8] + corr, y[8:gr]], axis=0)
            y = _silu(y)
            for e in range(2):
                ys = y[:, e * LANES:(e + 1) * LANES]
                hc = slice(((cb % 2) * 2 + e) * LANES, ((cb % 2) * 2 + e + 1) * LANES)
                if cb < 2:
                    qn_ref[bb, rows, hc] = l2n(ys) * (HEAD_DIM ** -0.5)
                elif cb < 4:
                    kn_ref[bb, rows, hc] = l2n(ys)
                else:
                    vv_ref[bb, rows, hc] = ys

        return [functools.partial(piece, cb) for cb in range(3 * GROUP_WIDTH // MXU_COLS)]

    ps = ps_ref[...]
    gate_ref[...] = -jnp.exp(alog_ref[...]) * _softplus(ps + dtb_ref[...])
    beta_ref[...] = 1.0 / (1.0 + jnp.exp(-ps))

    cpg = gr // CHUNK
    same, bd_tril, bd_strict = _chunk_block_masks(gr)
    tri_bd = jnp.where(bd_tril, 1.0, 0.0).astype(BF16)
    ones_bd = jnp.where(same, 1.0, 0.0).astype(BF16)
    eye = jnp.where(bd_tril, 1.0, 0.0) - jnp.where(bd_strict, 1.0, 0.0)
    ng = ng_ref[...]

    def setup(bb, gi):
        probs = []
        rows = slice(gi * gr, (gi + 1) * gr)
        g3 = _split3(gate_ref[bb, rows, :])
        dec = _dot(tri_bd, g3[0]) + _dot(tri_bd, g3[1]) + _dot(tri_bd, g3[2])
        dec_end = _dot(ones_bd, g3[0]) + _dot(ones_bd, g3[1]) + _dot(ones_bd, g3[2])
        dec_t = dec.T
        e_dec = jnp.exp(dec)
        e_end = jnp.exp(dec_end - dec)
        e_tot = jnp.exp(dec_end)
        beta = beta_ref[bb, rows, :]
        for h in range(N_HEADS):
            cols = slice(h * HEAD_DIM, (h + 1) * HEAD_DIM)
            la = SMALL_A + h
            diff = dec[:, la:la + 1] - dec_t[la:la + 1, :]
            dmask = jnp.where(bd_tril, jnp.exp(jnp.where(bd_tril, diff, 0.0)), 0.0)
            ecol = e_dec[:, la:la + 1]
            bcol = beta[:, SMALL_BETA + h:SMALL_BETA + h + 1]
            kn = kn_ref[bb, rows, cols]
            qn = qn_ref[bb, rows, cols]
            kb = kn * bcol
            kn_b = kn.astype(BF16)
            m = jnp.where(bd_strict, _dot_nt(kb.astype(BF16), kn_b) * dmask, 0.0)
            probs.append(dict(
                bb=bb, rows=rows, cols=cols, tm=eye - m, pw=m,
                qk_b=(_dot_nt(qn.astype(BF16), kn_b) * dmask).astype(BF16),
                rhs=jnp.concatenate([(vv_ref[bb, rows, cols] * bcol).astype(BF16), (kb * ecol).astype(BF16)],
                                    axis=1),
                qd_b=(qn * ecol).astype(BF16),
                ket_b=(kn * e_end[:, la:la + 1]).T.astype(BF16),
                d=[e_tot[ci * CHUNK:ci * CHUNK + 1, la:la + 1] for ci in range(cpg)]))
        return probs

    def inverse_stages(probs):
        def stage():
            for p in probs:
                pw_b = p["pw"].astype(BF16)
                p["pw"] = _dot(pw_b, pw_b)
            for p in probs:
                p["tm"] = p["tm"] + _dot(p["tm"].astype(BF16), p["pw"].astype(BF16))

        def finish():
            for p in probs:
                uw = _dot(p["tm"].astype(BF16), p["rhs"])
                p["u"] = uw[:, 0:HEAD_DIM]
                p["w_b"] = uw[:, HEAD_DIM:2 * HEAD_DIM].astype(BF16)

        return [stage] * 5 + [finish]

    zero_s = jnp.zeros((HEAD_DIM, HEAD_DIM), BF16)
    zero_v = jnp.zeros((CHUNK, 2 * HEAD_DIM), BF16)

    def lanes2(a, b):
        return jnp.concatenate([a, b], axis=1)

    def scan_group(pairs, states):
        v_news = [[] for _ in pairs]
        o_inters = [[] for _ in pairs]
        ops = [dict(u=lanes2(p0["u"], p1["u"]), w=lanes2(p0["w_b"], p1["w_b"]), qd=lanes2(p0["qd_b"], p1["qd_b"]))
               for p0, p1 in pairs]
        for ci in range(cpg):
            cr = slice(ci * CHUNK, (ci + 1) * CHUNK)
            pair = slice((ci // 2) * 2 * CHUNK, (ci // 2 + 1) * 2 * CHUNK)
            for k, (p0, p1) in enumerate(pairs):
                s_b = states[k].astype(BF16)
                bd_s = jnp.concatenate([lanes2(s_b[:, 0:HEAD_DIM], zero_s),
                                        lanes2(zero_s, s_b[:, HEAD_DIM:2 * HEAD_DIM])], axis=0)
                r = _dot(jnp.concatenate([ops[k]["w"][cr], ops[k]["qd"][cr]], axis=0), bd_s)
                v_new_b = (ops[k]["u"][cr] - r[0:CHUNK]).astype(BF16)
                o_inters[k].append(r[CHUNK:2 * CHUNK])
                v_news[k].append(v_new_b)
                v0 = lanes2(v_new_b[:, 0:HEAD_DIM], zero_v[:, 0:HEAD_DIM])
                v1 = lanes2(zero_v[:, 0:HEAD_DIM], v_new_b[:, HEAD_DIM:2 * HEAD_DIM])
                bd_v = jnp.concatenate([v0, zero_v, v1, zero_v] if ci % 2 == 0 else [zero_v, v0, zero_v, v1],
                                       axis=0)
                ket = lanes2(p0["ket_b"][:, pair], p1["ket_b"][:, pair])
                d_row = lanes2(jnp.broadcast_to(p0["d"][ci], (1, HEAD_DIM)),
                               jnp.broadcast_to(p1["d"][ci], (1, HEAD_DIM)))
                states[k] = states[k] * d_row + _dot(ket, bd_v)
        for k, pr in enumerate(pairs):
            o_inter = jnp.concatenate(o_inters[k], axis=0)
            v_all = jnp.concatenate(v_news[k], axis=0)
            for e, p in enumerate(pr):
                hl = slice(e * HEAD_DIM, (e + 1) * HEAD_DIM)
                o = o_inter[:, hl] + _dot(p["qk_b"], v_all[:, hl])
                zf = z_ref[p["bb"], p["rows"], p["cols"]].astype(F32)
                o_ref[p["bb"], p["rows"], p["cols"]] = (_rms(o, ng) * _silu(zf)).astype(BF16)

    for bb in range(nb):
        for gi in range(n_groups):
            for piece in conv_pieces(bb, gi):
                piece()
    probs = [[setup(bb, gi) for bb in range(nb)] for gi in range(n_groups)]
    for stage in inverse_stages([p for per_g in probs for per_b in per_g for p in per_b]):
        stage()
    states = [s_ref[bb, hp] for bb in range(nb) for hp in range(N_HEADS // 2)]
    for gi in range(n_groups):
        scan_group([(per_b[2 * hp], per_b[2 * hp + 1]) for per_b in probs[gi] for hp in range(N_HEADS // 2)],
                   states)
    for bb in range(nb):
        for e, src in enumerate((q_ref, k_ref, v_ref)):
            tail_ref[bb, :, e * GROUP_WIDTH:(e + 1) * GROUP_WIDTH] = src[bb, blk - GDN_TAIL:blk, :]
        for hp in range(N_HEADS // 2):
            s_ref[bb, hp] = states[bb * (N_HEADS // 2) + hp]


def _gdn(p3, ps3, cw, alog_row, dtb_row, ng, *, blk=512, nb=1):
    bsz, t, _ = p3.shape
    kern = functools.partial(_gdn_kernel, blk=blk)
    return pl.pallas_call(
        kern,
        grid=(bsz // nb, t // blk),
        in_specs=[pl.BlockSpec((nb, blk, GROUP_WIDTH), lambda b, i: (b, i, _blk("dqkv", GROUP_WIDTH))),
                  pl.BlockSpec((nb, blk, GROUP_WIDTH), lambda b, i: (b, i, _blk("dqkv", GROUP_WIDTH) + 1)),
                  pl.BlockSpec((nb, blk, GROUP_WIDTH), lambda b, i: (b, i, _blk("dqkv", GROUP_WIDTH) + 2)),
                  pl.BlockSpec((nb, blk, GROUP_WIDTH), lambda b, i: (b, i, _blk("dz", GROUP_WIDTH))),
                  pl.BlockSpec((nb, blk, LANES), lambda b, i: (b, i, 0)),
                  pl.BlockSpec((GDN_CONV, 3 * GROUP_WIDTH), lambda b, i: (0, 0)),
                  pl.BlockSpec((1, LANES), lambda b, i: (0, 0)),
                  pl.BlockSpec((1, LANES), lambda b, i: (0, 0)),
                  pl.BlockSpec((1, HEAD_DIM), lambda b, i: (0, 0))],
        out_specs=pl.BlockSpec((nb, blk, GROUP_WIDTH), lambda b, i: (b, i, 0)),
        out_shape=jax.ShapeDtypeStruct((bsz, t, GROUP_WIDTH), BF16),
        scratch_shapes=[pltpu.VMEM((nb, GDN_TAIL, 3 * GROUP_WIDTH), BF16),
                        pltpu.VMEM((nb, N_HEADS // 2, HEAD_DIM, 2 * HEAD_DIM), F32),
                        pltpu.VMEM((nb, blk, GROUP_WIDTH), F32),
                        pltpu.VMEM((nb, blk, GROUP_WIDTH), F32),
                        pltpu.VMEM((nb, blk, GROUP_WIDTH), F32),
                        pltpu.VMEM((nb, blk, LANES), F32),
                        pltpu.VMEM((nb, blk, LANES), F32)],
        compiler_params=pltpu.CompilerParams(dimension_semantics=("parallel", "arbitrary"),
                                             vmem_limit_bytes=VMEM_LIMIT),
        name="gdn",
    )(p3, p3, p3, p3, ps3, cw, alog_row, dtb_row, ng)


def _alibi_slopes():
    n = 2 * N_HEADS
    s = [2.0 ** (-8.0 * (i + 1) / n) for i in range(n)]
    return s[0::2], s[1::2]


def _pack_w_in(w_in):
    def ref_range(first, last):
        return w_in[..., _REF_OFF[first][0]:_REF_OFF[last][0] + _REF_OFF[last][1]]

    w_main = jnp.concatenate([ref_range("aq", "av"), ref_range("az", "dqkv"), ref_range("dz", "dz")],
                             axis=-1).astype(BF16)
    assert w_main.shape[-1] == MAIN_WIDTH
    small = [ref_range(n, n) for n in ("alr", "dbeta", "da")]
    n_small = sum(s.shape[-1] for s in small)
    w_small = jnp.concatenate(small + [jnp.zeros(w_in.shape[:-1] + (LANES - n_small,), w_in.dtype)],
                              axis=-1).astype(BF16)
    return w_main, w_small


def _lane_row(vals, start):
    depth, n = vals.shape
    return jnp.pad(vals.astype(F32), ((0, 0), (start, LANES - start - n)))[:, None, :]


def kernel(x, norm_g, w_in, w_out, gla_w_lr2, gla_b_lr2, gla_norm_g, swa_q_norm_g, swa_k_norm_g, swa_sinks,
           diff_q_norm_g, diff_k_norm_g, diff_lambda, diff_subln_g, gdn_conv_w, gdn_A_log, gdn_dt_bias,
           gdn_norm_g):
    bsz, t, d = x.shape
    depth = w_in.shape[0]
    slopes_b, slopes_c = _alibi_slopes()
    w_main, w_small = _pack_w_in(w_in)
    w_out_b = w_out.astype(BF16)
    wlr = jnp.pad(gla_w_lr2, ((0, 0), (0, LANES - GLA_RANK), (0, 0))).astype(BF16)
    alog_row = _lane_row(gdn_A_log, SMALL_A)
    dtb_row = _lane_row(gdn_dt_bias, SMALL_A)
    slopes_b_arr = jnp.asarray(slopes_b, F32)

    x2 = x.reshape(bsz * t, d)
    for l in range(depth):
        lambda_init = 0.8 - 0.6 * math.exp(-0.3 * l)
        p_main, p_small = _inproj(x2, norm_g[l][None, :], w_main, w_small, l)
        p3 = p_main.reshape(bsz, t, MAIN_WIDTH)
        ps3 = p_small.reshape(bsz, t, LANES)
        ya = _gla(p3, ps3, wlr[l], gla_b_lr2[l][None, :], gla_norm_g[l][None, :])
        yb = _swa(p3, slopes_b_arr, swa_sinks[l], swa_q_norm_g[l][None, :], swa_k_norm_g[l][None, :])
        qt, ka, vt = _diff_prep(p3, jnp.tile(diff_q_norm_g[l], 2)[None, :],
                                jnp.tile(diff_k_norm_g[l], 2)[None, :], tuple(slopes_c))
        yc = _diff_flash(qt, ka, vt, p3, diff_lambda[l], diff_subln_g[l][:, None], lambda_init)
        yd = _gdn(p3, ps3, gdn_conv_w[l], alog_row[l], dtb_row[l], gdn_norm_g[l][None, :])
        ys = [y.reshape(bsz * t, GROUP_WIDTH) for y in (ya, yb, yc, yd)]
        x2 = _outproj(ys, x2, w_out_b, l)
    return x2.reshape(bsz, t, d)
```
